```python
import jax, jax.numpy as jnp
from jax import lax
import numpy as np

D_MODEL = 1024
BATCH = 8
SEQ = 4096
DEPTH = 2

GRID_W = 64
CTX_LEN = 256
EPS = 1e-6
MLA_HEADS = 8
Q_LORA = 256
KV_LORA = 128
QK_NOPE = 64
QK_ROPE = 32
V_HEAD = 64
QK_HEAD = QK_NOPE + QK_ROPE
ROPE_BASE = 10000.0
Q_BLOCK = 128
FNET_GROUPS = 4
FNET_GROUP_DIM = 64
FNET_C = FNET_GROUPS * FNET_GROUP_DIM
CONV_C = 256
CONV_W = 31
N_BRANCH = 3
PEER_HEADS = 8
N_KEYS = 128
N_EXPERTS = N_KEYS * N_KEYS
D_KEY = 128
PEER_TOPK = 16
PEER_CHUNK = 128
KV_IN = KV_LORA + QK_ROPE
REST_SPLITS = [Q_LORA, Q_LORA + FNET_C, Q_LORA + FNET_C + 2 * CONV_C]
IN_COLS = KV_IN + Q_LORA + FNET_C + 2 * CONV_C + N_BRANCH * D_MODEL

kernel_name = 'hybrid_mla_fnet_conformer_peer_dit'


def rmsnorm(x, g):
    xf = x.astype(jnp.float32)
    y = xf * lax.rsqrt(jnp.mean(xf * xf, axis=-1, keepdims=True) + EPS)
    return (y * g.astype(jnp.float32)).astype(x.dtype)


def layernorm(x, g, b):
    xf = x.astype(jnp.float32)
    mu = jnp.mean(xf, axis=-1, keepdims=True)
    var = jnp.mean(jnp.square(xf - mu), axis=-1, keepdims=True)
    y = (xf - mu) * lax.rsqrt(var + EPS)
    return (y * g.astype(jnp.float32) + b.astype(jnp.float32)).astype(x.dtype)


def adaln(cond, w_mod, b_mod):
    return jnp.split(jax.nn.silu(cond) @ w_mod + b_mod, 6, axis=-1)


def modulate(h, shift, scale):
    return h * (1.0 + scale[:, None, :]) + shift[:, None, :]


def axial_rope_tables(n, dtype):
    rows = n // GRID_W
    row = jnp.repeat(jnp.arange(rows, dtype=jnp.float32), GRID_W)
    col = jnp.tile(jnp.arange(GRID_W, dtype=jnp.float32), rows)
    half = QK_ROPE // 2
    inv = ROPE_BASE ** (-jnp.arange(0, half, 2, dtype=jnp.float32) / half)
    ar = row[:, None] * inv
    ac = col[:, None] * inv
    ang = jnp.concatenate([ar, ar, ac, ac], axis=-1)
    return jnp.cos(ang).astype(dtype), jnp.sin(ang).astype(dtype)


def _rotate_half(z):
    z1, z2 = jnp.split(z, 2, axis=-1)
    return jnp.concatenate([-z2, z1], axis=-1)


def apply_axial_rope(z, cos, sin):
    zr, zc = jnp.split(z, 2, axis=-1)
    rot = jnp.concatenate([_rotate_half(zr), _rotate_half(zc)], axis=-1)
    return z * cos[:, None, :] + rot * sin[:, None, :]


def mla_q(cq, lp, rope):
    b, t, _ = cq.shape
    q = (rmsnorm(cq, lp['g_cq']) @ lp['w_uq']).reshape(b, t, MLA_HEADS, QK_HEAD)
    q = rmsnorm(q, lp['g_qn'])
    if rope is not None:
        q = jnp.concatenate([q[..., :QK_NOPE], apply_axial_rope(q[..., QK_NOPE:], *rope)], axis=-1)
    return q


def mla_kv(z, lp, rope):
    b, t, _ = z.shape
    ckv, kr = jnp.split(z, [KV_LORA], axis=-1)
    kv = (rmsnorm(ckv, lp['g_ckv']) @ lp['w_ukv']).reshape(b, t, MLA_HEADS, QK_NOPE + V_HEAD)
    k_nope, v = kv[..., :QK_NOPE], kv[..., QK_NOPE:]
    k_rope = jnp.broadcast_to(kr[:, :, None, :], (b, t, MLA_HEADS, QK_ROPE))
    k = rmsnorm(jnp.concatenate([k_nope, k_rope], axis=-1), lp['g_kn'])
    if rope is not None:
        k = jnp.concatenate([k[..., :QK_NOPE], apply_axial_rope(k[..., QK_NOPE:], *rope)], axis=-1)
    return k, v


def softmax_attend(q, k, v):
    s = jnp.einsum('bqhd,bkhd->bhqk', q, k).astype(jnp.float32) * (QK_HEAD ** -0.5)
    p = jax.nn.softmax(s, axis=-1).astype(v.dtype)
    return jnp.einsum('bhqk,bkhd->bqhd', p, v)


def latent_attention(q, k, v):
    b, s, h, _ = q.shape
    nb = s // Q_BLOCK
    qb = q.reshape(b, nb, Q_BLOCK, h, QK_HEAD).transpose(1, 0, 2, 3, 4)
    o = lax.map(lambda qi: softmax_attend(qi, k, v), qb)
    return o.transpose(1, 0, 2, 3, 4).reshape(b, s, h * V_HEAD)


def fourier_mix(z):
    b, t, _ = z.shape
    zz = z.astype(jnp.float32).reshape(b, t, FNET_GROUPS, FNET_GROUP_DIM)
    f = jnp.fft.fft2(zz, axes=(1, 3), norm='ortho').real
    return f.reshape(b, t, FNET_C).astype(z.dtype)


def conformer_conv(z, lp):
    a, g = jnp.split(z, 2, axis=-1)
    u = a * jax.nn.sigmoid(g)
    u = lax.conv_general_dilated(u, lp['w_dw'], window_strides=(1,), padding=[(CONV_W // 2, CONV_W // 2)],
                                 dimension_numbers=('NWC', 'WIO', 'NWC'), feature_group_count=CONV_C) + lp['b_dw']
    return jax.nn.silu(layernorm(u, lp['g_cln'], lp['b_cln']))


def branch_merge(a_o, zf, zc, zg, lp):
    b, t, _ = zg.shape
    g = jax.nn.sigmoid(zg.astype(jnp.float32)).astype(zg.dtype).reshape(b, t, N_BRANCH, D_MODEL)
    m = (g[:, :, 0] * (a_o @ lp['wb_attn'])
         + g[:, :, 1] * (fourier_mix(zf) @ lp['wb_fnet'])
         + g[:, :, 2] * (conformer_conv(zc, lp) @ lp['wb_conv']))
    return m @ lp['w_out']


def mixer_sublayer(x, ctx, mod_x, mod_c, lp, rope, update_ctx):
    sh, sc, gt = mod_x
    csh, csc, cgt = mod_c
    hx = modulate(rmsnorm(x, lp['g_norm1']), sh, sc)
    hc = modulate(rmsnorm(ctx, lp['g_norm1']), csh, csc)
    k_c, v_c = mla_kv(hc @ lp['w_in'][:, :KV_IN], lp, None)
    k_x, v_x = mla_kv(hx @ lp['w_in'][:, :KV_IN], lp, rope)
    cq_x, zf_x, zc_x, zg_x = jnp.split(hx @ lp['w_in'][:, KV_IN:], REST_SPLITS, axis=-1)
    q_x = mla_q(cq_x, lp, rope)
    a_x = latent_attention(q_x, jnp.concatenate([k_c, k_x], axis=1), jnp.concatenate([v_c, v_x], axis=1))
    x = x + gt[:, None, :] * branch_merge(a_x, zf_x, zc_x, zg_x, lp)
    if update_ctx:
        b, l, _ = ctx.shape
        cq_c, zf_c, zc_c, zg_c = jnp.split(hc @ lp['w_in'][:, KV_IN:], REST_SPLITS, axis=-1)
        a_c = softmax_attend(mla_q(cq_c, lp, None), k_c, v_c).reshape(b, l, MLA_HEADS * V_HEAD)
        ctx = ctx + cgt[:, None, :] * branch_merge(a_c, zf_c, zc_c, zg_c, lp)
    return x, ctx


def peer_ffn(h, w_query, sub_keys, u_tab, v_tab):
    b, t, d = h.shape
    chunks = h.reshape(-1, PEER_CHUNK, d)

    def chunk(hc):
        q = (hc @ w_query).reshape(PEER_CHUNK, PEER_HEADS, 2, D_KEY // 2)
        s = jnp.einsum('chpd,hpnd->chpn', q, sub_keys).astype(jnp.float32)
        s_top, i_top = lax.top_k(s, PEER_TOPK)
        cand = (s_top[:, :, 0, :, None] + s_top[:, :, 1, None, :]).reshape(PEER_CHUNK, PEER_HEADS, PEER_TOPK * PEER_TOPK)
        cand_id = (i_top[:, :, 0, :, None] * N_KEYS + i_top[:, :, 1, None, :]).reshape(PEER_CHUNK, PEER_HEADS, PEER_TOPK * PEER_TOPK)
        best, pos = lax.top_k(cand, PEER_TOPK)
        eid = jnp.take_along_axis(cand_id, pos, axis=-1)
        gate = jax.nn.softmax(best, axis=-1).astype(hc.dtype)
        u = jnp.take(u_tab, eid, axis=0)
        act = jax.nn.gelu(jnp.einsum('chkd,cd->chk', u, hc)) * gate
        v = jnp.take(v_tab, eid, axis=0)
        return jnp.einsum('chk,chkd->cd', act, v)

    return lax.map(chunk, chunks).reshape(b, t, d)


def setup_inputs(seed: int = 0) -> dict:
    key = jax.random.key(seed)
    ks = jax.random.split(key, 28)
    f32 = jnp.float32

    def nrm(k, shape, scale):
        return jax.random.normal(k, shape, f32) * scale

    def gain(k, shape):
        return 1.0 + 0.02 * jax.random.normal(k, shape, f32)

    L = DEPTH
    return {
        'x': nrm(ks[0], (BATCH, SEQ, D_MODEL), 1.0),
        'c': nrm(ks[1], (BATCH, D_MODEL), 1.0),
        'ctx': nrm(ks[2], (BATCH, CTX_LEN, D_MODEL), 1.0),
        'c_ctx': nrm(ks[3], (D_MODEL,), 1.0),
        'w_mod': nrm(ks[4], (L, D_MODEL, 6 * D_MODEL), 0.5 * D_MODEL ** -0.5),
        'b_mod': nrm(ks[5], (L, 6 * D_MODEL), 0.02),
        'g_norm1': gain(ks[6], (L, D_MODEL)),
        'w_in': nrm(ks[7], (L, D_MODEL, IN_COLS), D_MODEL ** -0.5),
        'g_ckv': gain(ks[8], (L, KV_LORA)),
        'w_ukv': nrm(ks[9], (L, KV_LORA, MLA_HEADS * (QK_NOPE + V_HEAD)), KV_LORA ** -0.5),
        'g_cq': gain(ks[10], (L, Q_LORA)),
        'w_uq': nrm(ks[11], (L, Q_LORA, MLA_HEADS * QK_HEAD), Q_LORA ** -0.5),
        'g_qn': gain(ks[12], (L, QK_HEAD)),
        'g_kn': gain(ks[13], (L, QK_HEAD)),
        'w_dw': nrm(ks[14], (L, CONV_W, 1, CONV_C), CONV_W ** -0.5),
        'b_dw': nrm(ks[15], (L, CONV_C), 0.02),
        'g_cln': gain(ks[16], (L, CONV_C)),
        'b_cln': nrm(ks[17], (L, CONV_C), 0.02),
        'wb_attn': nrm(ks[18], (L, MLA_HEADS * V_HEAD, D_MODEL), (MLA_HEADS * V_HEAD) ** -0.5),
        'wb_fnet': nrm(ks[19], (L, FNET_C, D_MODEL), FNET_C ** -0.5),
        'wb_conv': nrm(ks[20], (L, CONV_C, D_MODEL), CONV_C ** -0.5),
        'w_out': nrm(ks[21], (L, D_MODEL, D_MODEL), D_MODEL ** -0.5),
        'g_norm2': gain(ks[22], (L, D_MODEL)),
        'w_query': nrm(ks[23], (L, D_MODEL, PEER_HEADS * D_KEY), D_MODEL ** -0.5),
        'sub_keys': nrm(ks[24], (L, PEER_HEADS, 2, N_KEYS, D_KEY // 2), (D_KEY // 2) ** -0.5),
        'u_tab': nrm(ks[25], (L, N_EXPERTS, D_MODEL), D_MODEL ** -0.5),
        'v_tab': nrm(ks[26], (L, N_EXPERTS, D_MODEL), PEER_HEADS ** -0.5),
    }


def reference(x, c, ctx, c_ctx, w_mod, b_mod, g_norm1, w_in, g_ckv, w_ukv, g_cq, w_uq, g_qn, g_kn,
              w_dw, b_dw, g_cln, b_cln, wb_attn, wb_fnet, wb_conv, w_out, g_norm2, w_query, sub_keys,
              u_tab, v_tab):
    rope = axial_rope_tables(x.shape[1], x.dtype)
    for i in range(DEPTH):
        lp = {'g_norm1': g_norm1[i], 'w_in': w_in[i], 'g_ckv': g_ckv[i], 'w_ukv': w_ukv[i],
              'g_cq': g_cq[i], 'w_uq': w_uq[i], 'g_qn': g_qn[i], 'g_kn': g_kn[i],
              'w_dw': w_dw[i], 'b_dw': b_dw[i], 'g_cln': g_cln[i], 'b_cln': b_cln[i],
              'wb_attn': wb_attn[i], 'wb_fnet': wb_fnet[i], 'wb_conv': wb_conv[i], 'w_out': w_out[i]}
        sh1, sc1, gt1, sh2, sc2, gt2 = adaln(c, w_mod[i], b_mod[i])
        csh1, csc1, cgt1, csh2, csc2, cgt2 = adaln(c_ctx[None, :], w_mod[i], b_mod[i])
        update_ctx = i < DEPTH - 1
        x, ctx = mixer_sublayer(x, ctx, (sh1, sc1, gt1), (csh1, csc1, cgt1), lp, rope, update_ctx)
        x = x + gt2[:, None, :] * peer_ffn(modulate(rmsnorm(x, g_norm2[i]), sh2, sc2),
                                           w_query[i], sub_keys[i], u_tab[i], v_tab[i])
        if update_ctx:
            ctx = ctx + cgt2[:, None, :] * peer_ffn(modulate(rmsnorm(ctx, g_norm2[i]), csh2, csc2),
                                                    w_query[i], sub_keys[i], u_tab[i], v_tab[i])
    return x
```

```python
import functools

import numpy as np
import jax
import jax.numpy as jnp
from jax import lax
from jax.experimental import pallas as pl
from jax.experimental.pallas import tpu as pltpu

F32 = jnp.float32
BF16 = jnp.bfloat16
I32 = jnp.int32
U32 = jnp.uint32
HI = lax.Precision.HIGHEST

LANES = 128
SUBLANES = 8
VMEM_BYTES_V7X = 64 * 1024 * 1024

D = 1024
GRID_W = 64
EPS = 1e-6
HEADS = 8
Q_LORA = 256
KV_LORA = 128
QK_NOPE = 64
QK_ROPE = 32
V_HEAD = 64
QK_HEAD = QK_NOPE + QK_ROPE
ROPE_BASE = 10000.0
HP = LANES
FNET_GROUPS = 4
FNET_GROUP_DIM = 64
FNET_C = FNET_GROUPS * FNET_GROUP_DIM
CONV_C = 256
CONV_W = 31
CONV_PAD = 16
PEER_HEADS = 8
N_KEYS = 128
D_KEY = 128
PEER_TOPK = 16
PEER_SEL = PEER_HEADS * PEER_TOPK
KV_IN = KV_LORA + QK_ROPE
KV_IN_PAD = 2 * LANES
IN_SEGS = (KV_IN_PAD, Q_LORA, FNET_C, 2 * CONV_C, 3 * D)
FFT_N = 64
NEG = -1e30


def _params(sem, vmem_mb):
    return pltpu.CompilerParams(dimension_semantics=sem, vmem_limit_bytes=vmem_mb * 1024 * 1024)


def _rms(x, n):
    return x * lax.rsqrt(jnp.sum(x * x, axis=-1, keepdims=True) * (1.0 / n) + EPS)


def _adaln_kernel(c_ref, w_ref, b_ref, o_ref):
    c = c_ref[...]
    s = c * jax.nn.sigmoid(c)
    o_ref[0] = jnp.dot(s, w_ref[0], precision=HI, preferred_element_type=F32) + b_ref[0]


def _adaln(cond, w_mod, b_mod):
    depth = w_mod.shape[0]
    tn = 512
    return pl.pallas_call(
        _adaln_kernel,
        grid=(depth, 6 * D // tn),
        in_specs=[
            pl.BlockSpec((16, D), lambda l, j: (0, 0)),
            pl.BlockSpec((1, D, tn), lambda l, j: (l, 0, j)),
            pl.BlockSpec((1, 1, tn), lambda l, j: (l, 0, j)),
        ],
        out_specs=pl.BlockSpec((1, 16, tn), lambda l, j: (l, 0, j)),
        out_shape=jax.ShapeDtypeStruct((depth, 16, 6 * D), F32),
        compiler_params=_params(("arbitrary", "arbitrary"), 32),
        name="adaln",
    )(cond, w_mod, b_mod.reshape(depth, 1, 6 * D))


def _inproj_kernel(x_ref, mod_ref, g_ref, w_ref, *out_refs, widths):
    x = x_ref[0]
    y = _rms(x, D) * g_ref[...]
    h = (y * (1.0 + mod_ref[0, :, D:2 * D]) + mod_ref[0, :, 0:D]).astype(BF16)
    col = 0
    for o_ref, width in zip(out_refs, widths):
        for c0 in range(0, width, D):
            c1 = min(c0 + D, width)
            o_ref[0, :, c0:c1] = jnp.dot(h, w_ref[:, col + c0:col + c1], preferred_element_type=F32)
        col += width


def _inproj(x, mods, mod_row, g, w_pad, widths):
    b, s, _ = x.shape
    tm = 256
    ncol = sum(widths)
    return pl.pallas_call(
        functools.partial(_inproj_kernel, widths=widths),
        grid=(b, s // tm),
        in_specs=[
            pl.BlockSpec((1, tm, D), lambda bi, i: (bi, i, 0)),
            pl.BlockSpec((1, 1, 6 * D), lambda bi, i: (mod_row(bi), 0, 0)),
            pl.BlockSpec((1, D), lambda bi, i: (0, 0)),
            pl.BlockSpec((D, ncol), lambda bi, i: (0, 0)),
        ],
        out_specs=[pl.BlockSpec((1, tm, w), lambda bi, i: (bi, i, 0)) for w in widths],
        out_shape=[jax.ShapeDtypeStruct((b, s, w), F32) for w in widths],
        compiler_params=_params(("arbitrary", "arbitrary"), 48),
        name="inproj",
    )(x, mods, g.reshape(1, D), w_pad)


def _rope(x, cos, sin_a, sin_b):
    return x * cos + pltpu.roll(x, LANES - 8, 1) * sin_a + pltpu.roll(x, 8, 1) * sin_b


def _mla_prep_kernel(kv_ref, cq_ref, cos_ref, sa_ref, sb_ref, gckv_ref, gcq_ref, gqn_ref, gkn_ref,
                     wuk_ref, wuv_ref, wuq_ref, q_ref, k_ref, v_ref):
    kvx = kv_ref[0]
    ckv = (_rms(kvx[:, :KV_LORA], KV_LORA) * gckv_ref[...]).astype(BF16)
    k_nope = jnp.dot(ckv, wuk_ref[...], preferred_element_type=F32)
    v_ref[0] = jnp.dot(ckv, wuv_ref[...], preferred_element_type=F32).astype(BF16)
    k_rope = pltpu.roll(kvx[:, KV_LORA:], QK_NOPE, 1)
    cq = (_rms(cq_ref[0], Q_LORA) * gcq_ref[...]).astype(BF16)
    q = jnp.dot(cq, wuq_ref[...], preferred_element_type=F32)
    cos, sin_a, sin_b = cos_ref[...], sa_ref[...], sb_ref[...]
    for h in range(HEADS):
        sl = slice(h * HP, (h + 1) * HP)
        kh = k_nope[:, sl] + k_rope
        kh = _rms(kh, QK_HEAD) * gkn_ref[...]
        k_ref[0, :, sl] = _rope(kh, cos, sin_a, sin_b).astype(BF16)
        qh = _rms(q[:, sl], QK_HEAD) * gqn_ref[...]
        q_ref[0, :, sl] = _rope(qh, cos, sin_a, sin_b).astype(BF16)


def _mla_prep(kv, cq, tables, pos_block, lw):
    b, s, _ = kv.shape
    tm = 256
    full = lambda shape: pl.BlockSpec(shape, lambda bi, i: (0,) * len(shape))
    tok = lambda w: pl.BlockSpec((1, tm, w), lambda bi, i: (bi, i, 0))
    tab = pl.BlockSpec((tm, HP), lambda bi, i: (pos_block(i), 0))
    return pl.pallas_call(
        _mla_prep_kernel,
        grid=(b, s // tm),
        in_specs=[tok(KV_IN_PAD), tok(Q_LORA), tab, tab, tab,
                  full((1, KV_LORA)), full((1, Q_LORA)), full((1, HP)), full((1, HP)),
                  full((KV_LORA, HEADS * HP)), full((KV_LORA, HEADS * HP)), full((Q_LORA, HEADS * HP))],
        out_specs=[tok(HEADS * HP)] * 3,
        out_shape=[jax.ShapeDtypeStruct((b, s, HEADS * HP), BF16)] * 3,
        compiler_params=_params(("arbitrary", "arbitrary"), 40),
        name="mla_prep",
    )(kv, cq, *tables, lw['g_ckv'], lw['g_cq'], lw['g_qn'], lw['g_kn'], lw['w_uk'], lw['w_uv'], lw['w_uq'])


def _attn_kernel(q_ref, *refs, n_seg, chunk):
    o_ref = refs[2 * n_seg]
    tq = q_ref.shape[1]
    scale = QK_HEAD ** -0.5
    for h in range(HEADS):
        sl = slice(h * HP, (h + 1) * HP)
        qh = q_ref[0, :, sl]
        carry = (jnp.full((tq, 1), NEG, F32), jnp.zeros((tq, 1), F32), jnp.zeros((tq, HP), F32))
        for s in range(n_seg):
            k_ref, v_ref = refs[2 * s], refs[2 * s + 1]
            ck = min(chunk, k_ref.shape[1])

            def body(c, carry, k_ref=k_ref, v_ref=v_ref, ck=ck):
                m, l, acc = carry
                k0 = pl.multiple_of(c * ck, ck)
                kc = k_ref[0, pl.ds(k0, ck), sl]
                vc = v_ref[0, pl.ds(k0, ck), sl]
                sc = lax.dot_general(qh, kc, (((1,), (1,)), ((), ())), preferred_element_type=F32) * scale
                m_new = jnp.maximum(m, jnp.max(sc, axis=-1, keepdims=True))
                alpha = jnp.exp(m - m_new)
                p = jnp.exp(sc - m_new)
                l = alpha * l + jnp.sum(p, axis=-1, keepdims=True)
                acc = alpha * acc + jnp.dot(p.astype(BF16), vc, preferred_element_type=F32)
                return m_new, l, acc

            carry = lax.fori_loop(0, k_ref.shape[1] // ck, body, carry)
        _, l, acc = carry
        o_ref[0, :, sl] = acc / l


def _attention(q, kv_segs):
    b, sq, w = q.shape
    tq = 256
    in_specs = [pl.BlockSpec((1, tq, w), lambda bi, i: (bi, i, 0))]
    args = [q]
    for k, v in kv_segs:
        sk = k.shape[1]
        in_specs += [pl.BlockSpec((1, sk, w), lambda bi, i: (bi, 0, 0))] * 2
        args += [k, v]
    return pl.pallas_call(
        functools.partial(_attn_kernel, n_seg=len(kv_segs), chunk=512),
        grid=(b, sq // tq),
        in_specs=in_specs,
        out_specs=pl.BlockSpec((1, tq, w), lambda bi, i: (bi, i, 0)),
        out_shape=jax.ShapeDtypeStruct((b, sq, w), F32),
        compiler_params=_params(("arbitrary", "arbitrary"), 56),
        name="attention",
    )(*args)


def _dft_mats(n):
    k = np.arange(n)
    ang = 2.0 * np.pi * ((k[:, None] * k[None, :]) % n) / n
    return np.cos(ang).astype(np.float32), np.sin(ang).astype(np.float32)


def _channel_dft_mats():
    c, s = _dft_mats(FNET_GROUP_DIM)
    eye = np.eye(FNET_GROUPS, dtype=np.float32)
    return np.kron(eye, c), np.kron(eye, s)


def _fft1_kernel(x_ref, c_ref, s_ref, tr_ref, ti_ref, br_ref, bi_ref):
    x = x_ref[0]
    ar = jnp.dot(c_ref[...], x, precision=HI, preferred_element_type=F32)
    ai = -jnp.dot(s_ref[...], x, precision=HI, preferred_element_type=F32)
    tr, ti = tr_ref[...], ti_ref[...]
    br_ref[0] = ar * tr - ai * ti
    bi_ref[0] = ar * ti + ai * tr


def _fft2_kernel(br_ref, bi_ref, c_ref, s_ref, cc_ref, sc_ref, o_ref, *, nk, scale):
    c, s = c_ref[...], s_ref[...]
    dot = functools.partial(jnp.dot, precision=HI, preferred_element_type=F32)
    for j in range(nk):
        br, bi = br_ref[0, j], bi_ref[0, j]
        yr = dot(c, br) + dot(s, bi)
        yi = dot(c, bi) - dot(s, br)
        o_ref[0, :, j * FNET_C:(j + 1) * FNET_C] = (dot(yr, cc_ref[...]) + dot(yi, sc_ref[...])) * scale


def _fourier_latent(z):
    b, s, _ = z.shape
    assert s == FFT_N * FFT_N
    n = FFT_N
    cn, sn = _dft_mats(n)
    k1 = np.arange(n)
    ang = 2.0 * np.pi * ((k1[:, None] * k1[None, :]) % s) / s
    tr = jnp.repeat(jnp.asarray(np.cos(ang), F32), FNET_C, axis=1)
    ti = jnp.repeat(jnp.asarray(-np.sin(ang), F32), FNET_C, axis=1)
    wide = n * FNET_C
    tl = 2048
    mat = pl.BlockSpec((n, n), lambda bi, j: (0, 0))
    br, bi = pl.pallas_call(
        _fft1_kernel,
        grid=(b, wide // tl),
        in_specs=[pl.BlockSpec((1, n, tl), lambda bi, j: (bi, 0, j)), mat, mat,
                  pl.BlockSpec((n, tl), lambda bi, j: (0, j)), pl.BlockSpec((n, tl), lambda bi, j: (0, j))],
        out_specs=[pl.BlockSpec((1, n, tl), lambda bi, j: (bi, 0, j))] * 2,
        out_shape=[jax.ShapeDtypeStruct((b, n, wide), F32)] * 2,
        compiler_params=_params(("arbitrary", "arbitrary"), 32),
        name="fft_stage1",
    )(z.reshape(b, n, wide), cn, sn, tr, ti)
    cc, sc = _channel_dft_mats()
    nk = 8
    cmat = pl.BlockSpec((FNET_C, FNET_C), lambda bi, j: (0, 0))
    blk = pl.BlockSpec((1, nk, n, FNET_C), lambda bi, j: (bi, j, 0, 0))
    out = pl.pallas_call(
        functools.partial(_fft2_kernel, nk=nk, scale=float((s * FNET_GROUP_DIM) ** -0.5)),
        grid=(b, n // nk),
        in_specs=[blk, blk, mat, mat, cmat, cmat],
        out_specs=pl.BlockSpec((1, n, nk * FNET_C), lambda bi, j: (bi, 0, j)),
        out_shape=jax.ShapeDtypeStruct((b, n, wide), F32),
        compiler_params=_params(("arbitrary", "arbitrary"), 32),
        name="fft_stage2",
    )(br.reshape(b, n, n, FNET_C), bi.reshape(b, n, n, FNET_C), cn, sn, cc, sc)
    return out.reshape(b, s, FNET_C)


def _dft_direct_kernel(x_ref, c_ref, s_ref, cc_ref, sc_ref, o_ref, *, scale):
    dot = functools.partial(jnp.dot, precision=HI, preferred_element_type=F32)
    x = x_ref[0]
    yr = dot(c_ref[...], x)
    yi = -dot(s_ref[...], x)
    o_ref[0] = (dot(yr, cc_ref[...]) + dot(yi, sc_ref[...])) * scale


def _fourier_direct(z):
    b, s, _ = z.shape
    cn, sn = _dft_mats(s)
    cc, sc = _channel_dft_mats()
    mat = pl.BlockSpec((s, s), lambda bi: (0, 0))
    cmat = pl.BlockSpec((FNET_C, FNET_C), lambda bi: (0, 0))
    blk = pl.BlockSpec((1, s, FNET_C), lambda bi: (bi, 0, 0))
    return pl.pallas_call(
        functools.partial(_dft_direct_kernel, scale=float((s * FNET_GROUP_DIM) ** -0.5)),
        grid=(b,),
        in_specs=[blk, mat, mat, cmat, cmat],
        out_specs=blk,
        out_shape=jax.ShapeDtypeStruct((b, s, FNET_C), F32),
        compiler_params=_params(("arbitrary",), 32),
        name="dft_direct",
    )(z, cn, sn, cc, sc)


def _conv_kernel(z_ref, w_ref, bdw_ref, g_ref, b_ref, o_ref, u_ref, *, rows):
    s = z_ref.shape[1]
    zero = jnp.zeros((CONV_PAD, CONV_C), F32)
    u_ref[0:CONV_PAD, :] = zero
    u_ref[CONV_PAD + s:CONV_PAD + s + CONV_PAD, :] = zero
    for r0 in range(0, s, rows):
        a = z_ref[0, r0:r0 + rows, 0:CONV_C]
        g = z_ref[0, r0:r0 + rows, CONV_C:2 * CONV_C]
        u_ref[CONV_PAD + r0:CONV_PAD + r0 + rows, :] = a * jax.nn.sigmoid(g)
    off = CONV_PAD - CONV_W // 2
    for r0 in range(0, s, rows):
        acc = jnp.zeros((rows, CONV_C), F32) + bdw_ref[...]
        for j in range(CONV_W):
            acc = acc + u_ref[off + r0 + j:off + r0 + j + rows, :] * w_ref[j:j + 1, :]
        mu = jnp.mean(acc, axis=-1, keepdims=True)
        d = acc - mu
        var = jnp.mean(d * d, axis=-1, keepdims=True)
        y = d * lax.rsqrt(var + EPS) * g_ref[...] + b_ref[...]
        o_ref[0, r0:r0 + rows, :] = y * jax.nn.sigmoid(y)


def _conformer_conv(z, lw):
    b, s, _ = z.shape
    vec = pl.BlockSpec((1, CONV_C), lambda bi: (0, 0))
    return pl.pallas_call(
        functools.partial(_conv_kernel, rows=min(256, s)),
        grid=(b,),
        in_specs=[pl.BlockSpec((1, s, 2 * CONV_C), lambda bi: (bi, 0, 0)),
                  pl.BlockSpec((CONV_W + 1, CONV_C), lambda bi: (0, 0)), vec, vec, vec],
        out_specs=pl.BlockSpec((1, s, CONV_C), lambda bi: (bi, 0, 0)),
        out_shape=jax.ShapeDtypeStruct((b, s, CONV_C), F32),
        scratch_shapes=[pltpu.VMEM((s + 2 * CONV_PAD, CONV_C), F32)],
        compiler_params=_params(("arbitrary",), 48),
        name="conformer_conv",
    )(z, lw['w_dw'], lw['b_dw'], lw['g_cln'], lw['b_cln'])


def _merge_kernel(a_ref, f_ref, c_ref, zg_ref, x_ref, mod_ref, g2_ref, wa_ref, wf_ref, wc_ref, wo_ref,
                  xn_ref, h2_ref):
    dot = functools.partial(jnp.dot, preferred_element_type=F32)
    m = jax.nn.sigmoid(zg_ref[0, :, 0:D]) * dot(a_ref[0].astype(BF16), wa_ref[...])
    m = m + jax.nn.sigmoid(zg_ref[0, :, D:2 * D]) * dot(f_ref[0].astype(BF16), wf_ref[...])
    m = m + jax.nn.sigmoid(zg_ref[0, :, 2 * D:3 * D]) * dot(c_ref[0].astype(BF16), wc_ref[...])
    y = dot(m.astype(BF16), wo_ref[...])
    xn = x_ref[0] + mod_ref[0, :, 2 * D:3 * D] * y
    xn_ref[0] = xn
    h2_ref[0] = _rms(xn, D) * g2_ref[...] * (1.0 + mod_ref[0, :, 4 * D:5 * D]) + mod_ref[0, :, 3 * D:4 * D]


def _merge(a_o, fm, cv, zg, x, mods, mod_row, g2, lw):
    b, s, _ = x.shape
    tm = 256
    tok = lambda w: pl.BlockSpec((1, tm, w), lambda bi, i: (bi, i, 0))
    full = lambda r, c: pl.BlockSpec((r, c), lambda bi, i: (0, 0))
    return pl.pallas_call(
        _merge_kernel,
        grid=(b, s // tm),
        in_specs=[tok(HEADS * HP), tok(FNET_C), tok(CONV_C), tok(3 * D), tok(D),
                  pl.BlockSpec((1, 1, 6 * D), lambda bi, i: (mod_row(bi), 0, 0)), full(1, D),
                  full(HEADS * HP, D), full(FNET_C, D), full(CONV_C, D), full(D, D)],
        out_specs=[tok(D), tok(D)],
        out_shape=[jax.ShapeDtypeStruct((b, s, D), F32)] * 2,
        compiler_params=_params(("arbitrary", "arbitrary"), 48),
        name="merge",
    )(a_o, fm, cv, zg, x, mods, g2.reshape(1, D), lw['wb_attn'], lw['wb_fnet'], lw['wb_conv'], lw['w_out'])


def _top_rows(s, k):
    r = s.shape[0]
    rows = lax.broadcasted_iota(I32, s.shape, 0)
    vals, idxs = [], []
    for _ in range(k):
        m = jnp.max(s, axis=0, keepdims=True)
        i = jnp.min(jnp.where(s == m, rows, r), axis=0, keepdims=True)
        vals.append(m)
        idxs.append(i)
        s = jnp.where(rows == i, -jnp.inf, s)
    return jnp.concatenate(vals, axis=0), jnp.concatenate(idxs, axis=0)


def _peer_score_kernel(h_ref, wq_ref, keys_ref, eid_ref, gate_ref, qt_ref, ts_ref, ti_ref):
    tn = h_ref.shape[0]
    half = D_KEY // 2
    qt_ref[...] = lax.dot_general(wq_ref[...], h_ref[...], (((1,), (1,)), ((), ())), precision=HI,
                                  preferred_element_type=F32)

    def sub_scores(hp, carry):
        r0 = pl.multiple_of(hp * half, half)
        for lt in range(tn // LANES):
            ls = slice(lt * LANES, (lt + 1) * LANES)
            s = jnp.dot(keys_ref[hp], qt_ref[pl.ds(r0, half), ls], precision=HI, preferred_element_type=F32)
            v, i = _top_rows(s, PEER_TOPK)
            ts_ref[hp, :, ls] = v
            ti_ref[hp, :, ls] = i
        return carry

    lax.fori_loop(0, 2 * PEER_HEADS, sub_scores, 0)

    def head(h, carry):
        for lt in range(tn // LANES):
            ls = slice(lt * LANES, (lt + 1) * LANES)
            s1, s2 = ts_ref[2 * h, :, ls], ts_ref[2 * h + 1, :, ls]
            i1, i2 = ti_ref[2 * h, :, ls], ti_ref[2 * h + 1, :, ls]
            cand = jnp.concatenate([s1[a:a + 1, :] + s2 for a in range(PEER_TOPK)], axis=0)
            cid = jnp.concatenate([i1[a:a + 1, :] * N_KEYS + i2 for a in range(PEER_TOPK)], axis=0)
            best, pos = _top_rows(cand, PEER_TOPK)
            rows = lax.broadcasted_iota(I32, cand.shape, 0)
            eid = jnp.concatenate(
                [jnp.max(jnp.where(rows == pos[a:a + 1, :], cid, -1), axis=0, keepdims=True)
                 for a in range(PEER_TOPK)], axis=0)
            e = jnp.exp(best - best[0:1, :])
            eid_ref[h, :, ls] = eid
            gate_ref[h, :, ls] = e / jnp.sum(e, axis=0, keepdims=True)
        return carry

    lax.fori_loop(0, PEER_HEADS, head, 0)


def _peer_scores(h2, wq_t, keys):
    t = h2.shape[0]
    tn = 256
    out_blk = pl.BlockSpec((PEER_HEADS, PEER_TOPK, tn), lambda i: (0, 0, i))
    return pl.pallas_call(
        _peer_score_kernel,
        grid=(t // tn,),
        in_specs=[pl.BlockSpec((tn, D), lambda i: (i, 0)),
                  pl.BlockSpec((PEER_HEADS * D_KEY, D), lambda i: (0, 0)),
                  pl.BlockSpec((2 * PEER_HEADS, N_KEYS, D_KEY // 2), lambda i: (0, 0, 0))],
        out_specs=[out_blk, out_blk],
        out_shape=[jax.ShapeDtypeStruct((PEER_HEADS, PEER_TOPK, t), I32),
                   jax.ShapeDtypeStruct((PEER_HEADS, PEER_TOPK, t), F32)],
        scratch_shapes=[pltpu.VMEM((PEER_HEADS * D_KEY, tn), F32),
                        pltpu.VMEM((2 * PEER_HEADS, PEER_TOPK, tn), F32),
                        pltpu.VMEM((2 * PEER_HEADS, PEER_TOPK, tn), I32)],
        compiler_params=_params(("arbitrary",), 32),
        name="peer_scores",
    )(h2, wq_t, keys)


HALF_SUB = SUBLANES // 2
BITREV8 = (0, 4, 2, 6, 1, 5, 3, 7)
PEER_TN = 128


def _unpack(word):
    lo = lax.bitcast_convert_type(word << 16, F32)
    hi = lax.bitcast_convert_type(word & jnp.uint32(0xFFFF0000), F32)
    return lo, hi


def _sublane_totals(ps):
    sub = lax.broadcasted_iota(I32, (SUBLANES, LANES), 0)
    level = ps
    for shift in (4, 2, 1):
        keep = (sub & shift) == 0
        nxt = []
        for a, b in zip(level[0::2], level[1::2]):
            x = jnp.where(keep, a, pltpu.roll(b, shift, 0))
            y = jnp.where(keep, pltpu.roll(a, SUBLANES - shift, 0), b)
            nxt.append(x + y)
        level = nxt
    return level[0]


def _load_tile_indices(eid_hbm, eid_smem, sem):
    i = pl.program_id(0)
    cp = pltpu.make_async_copy(eid_hbm.at[pl.ds(i * PEER_TN, PEER_TN)], eid_smem, sem)
    cp.start()
    cp.wait()


def _load_table_once(tab_hbm, tab_vmem, sem):
    @pl.when(pl.program_id(0) == 0)
    def _():
        cp = pltpu.make_async_copy(tab_hbm, tab_vmem, sem)
        cp.start()
        cp.wait()


def _peer_u_kernel(eid_hbm, tab_hbm, h3_ref, gate_ref, eidt_ref, we_ref, wo_ref,
                   tab_vmem, eid_smem, hs_ref, acc_ref, sems):
    _load_table_once(tab_hbm, tab_vmem, sems.at[0])
    _load_tile_indices(eid_hbm, eid_smem, sems.at[1])
    low = lax.broadcasted_iota(I32, (SUBLANES, LANES), 0) < HALF_SUB
    lane = lax.broadcasted_iota(I32, (PEER_SEL, PEER_TN), 1)
    acc_ref[...] = jnp.zeros((PEER_SEL, PEER_TN), F32)

    def token(t, carry):
        h = h3_ref[t]
        hr = pltpu.roll(h, HALF_SUB, 0)
        zero = jnp.zeros_like(h)
        hs_ref[0] = jnp.where(low, h, zero)
        hs_ref[1] = jnp.where(low, zero, hr)
        hs_ref[2] = jnp.where(low, hr, zero)
        hs_ref[3] = jnp.where(low, zero, h)
        cols = []
        for g in range(PEER_SEL // SUBLANES):
            ps = [None] * SUBLANES
            for r in range(SUBLANES):
                e = eid_smem[t, g * SUBLANES + r]
                par = e & 1
                lo, hi = _unpack(tab_vmem[lax.shift_right_logical(e, 1)])
                ps[BITREV8[r]] = lo * hs_ref[par] + hi * hs_ref[2 + par]
            cols.append(_sublane_totals(ps))
        z = jnp.sum(jnp.concatenate(cols, axis=0), axis=1, keepdims=True)
        acc_ref[...] = jnp.where(lane == t, z, acc_ref[...])
        return carry

    lax.fori_loop(0, PEER_TN, token, 0)
    w = jax.nn.gelu(acc_ref[...]) * gate_ref[...]
    we = jnp.where((eidt_ref[...] & 1) == 0, w, 0.0)
    we_ref[...] = we
    wo_ref[...] = w - we


def _peer_u(eid_tok, eid_sel, gate_sel, u_packed, h3):
    t = eid_tok.shape[0]
    sel = pl.BlockSpec((PEER_SEL, PEER_TN), lambda i: (0, i))
    return pl.pallas_call(
        _peer_u_kernel,
        grid=(t // PEER_TN,),
        in_specs=[pl.BlockSpec(memory_space=pl.ANY), pl.BlockSpec(memory_space=pl.ANY),
                  pl.BlockSpec((PEER_TN, SUBLANES, LANES), lambda i: (i, 0, 0)), sel, sel],
        out_specs=[sel, sel],
        out_shape=[jax.ShapeDtypeStruct((PEER_SEL, t), F32)] * 2,
        scratch_shapes=[pltpu.VMEM(u_packed.shape, U32), pltpu.SMEM((PEER_TN, PEER_SEL), I32),
                        pltpu.VMEM((4, SUBLANES, LANES), F32), pltpu.VMEM((PEER_SEL, PEER_TN), F32),
                        pltpu.SemaphoreType.DMA((2,))],
        compiler_params=_params(("arbitrary",), 48),
        name="peer_u",
    )(eid_tok, u_packed, h3, gate_sel, eid_sel)


def _peer_v_kernel(eid_hbm, we_hbm, wo_hbm, tab_hbm, x3_ref, gt_ref, o_ref,
                   tab_vmem, eid_smem, we_smem, wo_smem, sems):
    _load_table_once(tab_hbm, tab_vmem, sems.at[0])
    _load_tile_indices(eid_hbm, eid_smem, sems.at[1])
    _load_tile_indices(we_hbm, we_smem, sems.at[2])
    _load_tile_indices(wo_hbm, wo_smem, sems.at[3])
    low = lax.broadcasted_iota(I32, (SUBLANES, LANES), 0) < HALF_SUB
    n_acc = 4

    def token(t, carry):
        acc_lo = [jnp.zeros((SUBLANES, LANES), F32) for _ in range(n_acc)]
        acc_hi = [jnp.zeros((SUBLANES, LANES), F32) for _ in range(n_acc)]
        for j in range(PEER_SEL):
            e = eid_smem[t, j]
            wv = jnp.where(low, we_smem[t, j], wo_smem[t, j])
            lo, hi = _unpack(tab_vmem[lax.shift_right_logical(e, 1)])
            acc_lo[j % n_acc] = acc_lo[j % n_acc] + wv * lo
            acc_hi[j % n_acc] = acc_hi[j % n_acc] + wv * hi
        lo = (acc_lo[0] + acc_lo[1]) + (acc_lo[2] + acc_lo[3])
        hi = (acc_hi[0] + acc_hi[1]) + (acc_hi[2] + acc_hi[3])
        lo = lo + pltpu.roll(lo, HALF_SUB, 0)
        hi = hi + pltpu.roll(hi, HALF_SUB, 0)
        o_ref[t] = x3_ref[t] + gt_ref[0] * jnp.where(low, lo, hi)
        return carry

    lax.fori_loop(0, PEER_TN, token, 0)


def _peer_v(eid_tok, we_tok, wo_tok, v_packed, x3, gt3, gate_row):
    t = eid_tok.shape[0]
    any_spec = pl.BlockSpec(memory_space=pl.ANY)
    tok = pl.BlockSpec((PEER_TN, SUBLANES, LANES), lambda i: (i, 0, 0))
    return pl.pallas_call(
        _peer_v_kernel,
        grid=(t // PEER_TN,),
        in_specs=[any_spec, any_spec, any_spec, any_spec, tok,
                  pl.BlockSpec((1, SUBLANES, LANES), lambda i: (gate_row(i), 0, 0))],
        out_specs=tok,
        out_shape=jax.ShapeDtypeStruct(x3.shape, F32),
        scratch_shapes=[pltpu.VMEM(v_packed.shape, U32), pltpu.SMEM((PEER_TN, PEER_SEL), I32),
                        pltpu.SMEM((PEER_TN, PEER_SEL), F32), pltpu.SMEM((PEER_TN, PEER_SEL), F32),
                        pltpu.SemaphoreType.DMA((4,))],
        compiler_params=_params(("arbitrary",), 48),
        name="peer_v",
    )(eid_tok, we_tok, wo_tok, v_packed, x3, gt3)


def _pack_table(tab):
    n = tab.shape[0]
    bits = lax.bitcast_convert_type(tab.astype(BF16), jnp.uint16).astype(U32)
    words = bits[:, :D // 2] | (bits[:, D // 2:] << 16)
    return words.reshape(n // 2, SUBLANES, LANES)


def _peer(xn, h2, gt3, gate_row, pw):
    b, s, _ = xn.shape
    t = b * s
    eid_sel, gate_sel = _peer_scores(h2.reshape(t, D), pw['wq_t'], pw['keys'])
    eid_sel = eid_sel.reshape(PEER_SEL, t)
    gate_sel = gate_sel.reshape(PEER_SEL, t)
    eid_tok = eid_sel.T
    we, wo = _peer_u(eid_tok, eid_sel, gate_sel, pw['u_packed'], h2.reshape(t, SUBLANES, LANES))
    out = _peer_v(eid_tok, we.T, wo.T, pw['v_packed'], xn.reshape(t, SUBLANES, LANES), gt3,
                  functools.partial(gate_row, tiles_per_seq=s // PEER_TN))
    return out.reshape(b, s, D)


def _rope_tables(s):
    pos = np.arange(s)
    half = QK_ROPE // 2
    inv = ROPE_BASE ** (-np.arange(0, half, 2, dtype=np.float64) / half)
    ar = (pos // GRID_W)[:, None] * inv
    ac = (pos % GRID_W)[:, None] * inv
    ang = np.concatenate([ar, ar, ac, ac], axis=-1)
    cos = np.ones((s, HP), np.float32)
    sin_a = np.zeros((s, HP), np.float32)
    sin_b = np.zeros((s, HP), np.float32)
    cos[:, QK_NOPE:QK_HEAD] = np.cos(ang)
    sin = np.sin(ang)
    first = (np.arange(QK_ROPE) % (half)) < half // 2
    sin_a[:, QK_NOPE:QK_HEAD] = np.where(first, -sin, 0.0)
    sin_b[:, QK_NOPE:QK_HEAD] = np.where(first, 0.0, sin)
    return jnp.asarray(cos), jnp.asarray(sin_a), jnp.asarray(sin_b)


def _identity_tables(s):
    return jnp.ones((s, HP), F32), jnp.zeros((s, HP), F32), jnp.zeros((s, HP), F32)


def _pad_heads(w, width):
    k = w.shape[0]
    w = w.reshape(k, HEADS, width)
    return jnp.pad(w, ((0, 0), (0, 0), (0, HP - width))).reshape(k, HEADS * HP)


def _pad_lanes(g):
    return jnp.pad(g, (0, HP - g.shape[0])).reshape(1, HP)


def _layer_weights(i, w_in, g_ckv, w_ukv, g_cq, w_uq, g_qn, g_kn, w_dw, b_dw, g_cln, b_cln,
                   wb_attn, wb_fnet, wb_conv, w_out):
    win = w_in[i]
    w_pad = jnp.concatenate([win[:, :KV_IN], jnp.zeros((D, KV_IN_PAD - KV_IN), F32), win[:, KV_IN:]], axis=1)
    ukv = w_ukv[i].reshape(KV_LORA, HEADS, QK_NOPE + V_HEAD)
    wb_a = wb_attn[i].reshape(HEADS, V_HEAD, D)
    return {
        'w_in': w_pad.astype(BF16),
        'g_ckv': g_ckv[i].reshape(1, KV_LORA), 'g_cq': g_cq[i].reshape(1, Q_LORA),
        'g_qn': _pad_lanes(g_qn[i]), 'g_kn': _pad_lanes(g_kn[i]),
        'w_uk': _pad_heads(ukv[:, :, :QK_NOPE].reshape(KV_LORA, HEADS * QK_NOPE), QK_NOPE).astype(BF16),
        'w_uv': _pad_heads(ukv[:, :, QK_NOPE:].reshape(KV_LORA, HEADS * V_HEAD), V_HEAD).astype(BF16),
        'w_uq': _pad_heads(w_uq[i], QK_HEAD).astype(BF16),
        'w_dw': jnp.pad(w_dw[i].reshape(CONV_W, CONV_C), ((0, 1), (0, 0))),
        'b_dw': b_dw[i].reshape(1, CONV_C), 'g_cln': g_cln[i].reshape(1, CONV_C),
        'b_cln': b_cln[i].reshape(1, CONV_C),
        'wb_attn': jnp.pad(wb_a, ((0, 0), (0, HP - V_HEAD), (0, 0))).reshape(HEADS * HP, D).astype(BF16),
        'wb_fnet': wb_fnet[i].astype(BF16), 'wb_conv': wb_conv[i].astype(BF16), 'w_out': w_out[i].astype(BF16),
    }


def _mixer_branches(zf, zc, lw):
    fm = _fourier_latent(zf) if zf.shape[1] == FFT_N * FFT_N else _fourier_direct(zf)
    return fm, _conformer_conv(zc, lw)


def kernel(x, c, ctx, c_ctx, w_mod, b_mod, g_norm1, w_in, g_ckv, w_ukv, g_cq, w_uq, g_qn, g_kn, w_dw, b_dw,
           g_cln, b_cln, wb_attn, wb_fnet, wb_conv, w_out, g_norm2, w_query, sub_keys, u_tab, v_tab):
    b, s, _ = x.shape
    n_ctx = ctx.shape[1]
    depth = w_mod.shape[0]
    ctx_row = b
    cond = jnp.concatenate([c, c_ctx[None, :], jnp.zeros((16 - b - 1, D), F32)], axis=0)
    mods_all = _adaln(cond, w_mod, b_mod)
    rope = _rope_tables(s)
    no_rope = _identity_tables(n_ctx)
    x_row = lambda bi: bi
    c_row = lambda bi: ctx_row
    x_gate_row = lambda i, tiles_per_seq: i // tiles_per_seq
    c_gate_row = lambda i, tiles_per_seq: ctx_row
    pos_tiles = s // 256
    for i in range(depth):
        lw = _layer_weights(i, w_in, g_ckv, w_ukv, g_cq, w_uq, g_qn, g_kn, w_dw, b_dw, g_cln, b_cln,
                            wb_attn, wb_fnet, wb_conv, w_out)
        pw = {'wq_t': w_query[i].T, 'keys': sub_keys[i].reshape(2 * PEER_HEADS, N_KEYS, D_KEY // 2),
              'u_packed': _pack_table(u_tab[i]), 'v_packed': _pack_table(v_tab[i])}
        mods = mods_all[i].reshape(16, 1, 6 * D)
        gt3 = mods_all[i][:, 5 * D:6 * D].reshape(16, SUBLANES, LANES)
        update_ctx = i < depth - 1

        kv_x, cq_x, zf_x, zc_x, zg_x = _inproj(x, mods, x_row, g_norm1[i], lw['w_in'], IN_SEGS)
        if update_ctx:
            kv_c, cq_c, zf_c, zc_c, zg_c = _inproj(ctx, mods, c_row, g_norm1[i], lw['w_in'], IN_SEGS)
        else:
            kv_c, cq_c = _inproj(ctx, mods, c_row, g_norm1[i], lw['w_in'][:, :KV_IN_PAD + Q_LORA], IN_SEGS[:2])
        q_c, k_c, v_c = _mla_prep(kv_c, cq_c, no_rope, lambda t: 0, lw)
        q_x, k_x, v_x = _mla_prep(kv_x, cq_x, rope, lambda t: t % pos_tiles, lw)

        a_x = _attention(q_x, [(k_c, v_c), (k_x, v_x)])
        fm_x, cv_x = _mixer_branches(zf_x, zc_x, lw)
        xn, h2 = _merge(a_x, fm_x, cv_x, zg_x, x, mods, x_row, g_norm2[i], lw)
        x = _peer(xn, h2, gt3, x_gate_row, pw)
        if update_ctx:
            a_c = _attention(q_c, [(k_c, v_c)])
            fm_c, cv_c = _mixer_branches(zf_c, zc_c, lw)
            cn, ch2 = _merge(a_c, fm_c, cv_c, zg_c, ctx, mods, c_row, g_norm2[i], lw)
            ctx = _peer(cn, ch2, gt3, c_gate_row, pw)
    return x
```

```python
import functools

import numpy as np
import jax
import jax.numpy as jnp
from jax import lax
from jax.experimental import pallas as pl
from jax.experimental.pallas import tpu as pltpu

F32 = jnp.float32
BF16 = jnp.bfloat16
I32 = jnp.int32
U32 = jnp.uint32
HI = lax.Precision.HIGHEST

LANES = 128
SUBLANES = 8
VMEM_BYTES_V7X = 64 * 1024 * 1024

D = 1024
GRID_W = 64
EPS = 1e-6
HEADS = 8
Q_LORA = 256
KV_LORA = 128
QK_NOPE = 64
QK_ROPE = 32
V_HEAD = 64
QK_HEAD = QK_NOPE + QK_ROPE
ROPE_BASE = 10000.0
HP = LANES
FNET_GROUPS = 4
FNET_GROUP_DIM = 64
FNET_C = FNET_GROUPS * FNET_GROUP_DIM
CONV_C = 256
CONV_W = 31
CONV_PAD = 16
PEER_HEADS = 8
N_KEYS = 128
D_KEY = 128
PEER_TOPK = 16
PEER_SEL = PEER_HEADS * PEER_TOPK
KV_IN = KV_LORA + QK_ROPE
KV_IN_PAD = 2 * LANES
IN_SEGS = (KV_IN_PAD, Q_LORA, FNET_C, 2 * CONV_C, 3 * D)
FFT_N = 64
NEG = -1e30


def _params(sem, vmem_mb):
    return pltpu.CompilerParams(dimension_semantics=sem, vmem_limit_bytes=vmem_mb * 1024 * 1024)


def _rms(x, n):
    return x * lax.rsqrt(jnp.sum(x * x, axis=-1, keepdims=True) * (1.0 / n) + EPS)


def _adaln_kernel(c_ref, w_ref, b_ref, o_ref):
    c = c_ref[...]
    s = c * jax.nn.sigmoid(c)
    o_ref[0] = jnp.dot(s, w_ref[0], precision=HI, preferred_element_type=F32) + b_ref[0]


def _adaln(cond, w_mod, b_mod):
    depth = w_mod.shape[0]
    tn = 512
    return pl.pallas_call(
        _adaln_kernel,
        grid=(depth, 6 * D // tn),
        in_specs=[
            pl.BlockSpec((16, D), lambda l, j: (0, 0)),
            pl.BlockSpec((1, D, tn), lambda l, j: (l, 0, j)),
            pl.BlockSpec((1, 1, tn), lambda l, j: (l, 0, j)),
        ],
        out_specs=pl.BlockSpec((1, 16, tn), lambda l, j: (l, 0, j)),
        out_shape=jax.ShapeDtypeStruct((depth, 16, 6 * D), F32),
        compiler_params=_params(("arbitrary", "arbitrary"), 32),
        name="adaln",
    )(cond, w_mod, b_mod.reshape(depth, 1, 6 * D))


def _inproj_kernel(x_ref, mod_ref, g_ref, w_ref, *out_refs, widths):
    x = x_ref[0]
    y = _rms(x, D) * g_ref[...]
    h = (y * (1.0 + mod_ref[0, :, D:2 * D]) + mod_ref[0, :, 0:D]).astype(BF16)
    col = 0
    for o_ref, width in zip(out_refs, widths):
        for c0 in range(0, width, D):
            c1 = min(c0 + D, width)
            o_ref[0, :, c0:c1] = jnp.dot(h, w_ref[:, col + c0:col + c1], preferred_element_type=F32)
        col += width


def _inproj(x, mods, mod_row, g, w_pad, widths):
    b, s, _ = x.shape
    tm = 256
    ncol = sum(widths)
    return pl.pallas_call(
        functools.partial(_inproj_kernel, widths=widths),
        grid=(b, s // tm),
        in_specs=[
            pl.BlockSpec((1, tm, D), lambda bi, i: (bi, i, 0)),
            pl.BlockSpec((1, 1, 6 * D), lambda bi, i: (mod_row(bi), 0, 0)),
            pl.BlockSpec((1, D), lambda bi, i: (0, 0)),
            pl.BlockSpec((D, ncol), lambda bi, i: (0, 0)),
        ],
        out_specs=[pl.BlockSpec((1, tm, w), lambda bi, i: (bi, i, 0)) for w in widths],
        out_shape=[jax.ShapeDtypeStruct((b, s, w), F32) for w in widths],
        compiler_params=_params(("arbitrary", "arbitrary"), 48),
        name="inproj",
    )(x, mods, g.reshape(1, D), w_pad)


def _rope(x, cos, sin_a, sin_b):
    return x * cos + pltpu.roll(x, LANES - 8, 1) * sin_a + pltpu.roll(x, 8, 1) * sin_b


def _mla_prep_kernel(kv_ref, cq_ref, cos_ref, sa_ref, sb_ref, gckv_ref, gcq_ref, gqn_ref, gkn_ref,
                     wuk_ref, wuv_ref, wuq_ref, q_ref, k_ref, v_ref):
    kvx = kv_ref[0]
    ckv = (_rms(kvx[:, :KV_LORA], KV_LORA) * gckv_ref[...]).astype(BF16)
    k_nope = jnp.dot(ckv, wuk_ref[...], preferred_element_type=F32)
    v_ref[0] = jnp.dot(ckv, wuv_ref[...], preferred_element_type=F32).astype(BF16)
    k_rope = pltpu.roll(kvx[:, KV_LORA:], QK_NOPE, 1)
    cq = (_rms(cq_ref[0], Q_LORA) * gcq_ref[...]).astype(BF16)
    q = jnp.dot(cq, wuq_ref[...], preferred_element_type=F32)
    cos, sin_a, sin_b = cos_ref[...], sa_ref[...], sb_ref[...]
    for h in range(HEADS):
        sl = slice(h * HP, (h + 1) * HP)
        kh = k_nope[:, sl] + k_rope
        kh = _rms(kh, QK_HEAD) * gkn_ref[...]
        k_ref[0, :, sl] = _rope(kh, cos, sin_a, sin_b).astype(BF16)
        qh = _rms(q[:, sl], QK_HEAD) * gqn_ref[...]
        q_ref[0, :, sl] = (_rope(qh, cos, sin_a, sin_b) * QK_HEAD ** -0.5).astype(BF16)


def _mla_prep(kv, cq, tables, pos_block, lw):
    b, s, _ = kv.shape
    tm = 256
    full = lambda shape: pl.BlockSpec(shape, lambda bi, i: (0,) * len(shape))
    tok = lambda w: pl.BlockSpec((1, tm, w), lambda bi, i: (bi, i, 0))
    tab = pl.BlockSpec((tm, HP), lambda bi, i: (pos_block(i), 0))
    return pl.pallas_call(
        _mla_prep_kernel,
        grid=(b, s // tm),
        in_specs=[tok(KV_IN_PAD), tok(Q_LORA), tab, tab, tab,
                  full((1, KV_LORA)), full((1, Q_LORA)), full((1, HP)), full((1, HP)),
                  full((KV_LORA, HEADS * HP)), full((KV_LORA, HEADS * HP)), full((Q_LORA, HEADS * HP))],
        out_specs=[tok(HEADS * HP)] * 3,
        out_shape=[jax.ShapeDtypeStruct((b, s, HEADS * HP), BF16)] * 3,
        compiler_params=_params(("arbitrary", "arbitrary"), 40),
        name="mla_prep",
    )(kv, cq, *tables, lw['g_ckv'], lw['g_cq'], lw['g_qn'], lw['g_kn'], lw['w_uk'], lw['w_uv'], lw['w_uq'])


def _attn_kernel(q_ref, *refs, n_seg, chunk, hgroup):
    o_ref = refs[2 * n_seg]
    tq = q_ref.shape[1]
    for h0 in range(0, HEADS, hgroup):
        sls = [slice(h * HP, (h + 1) * HP) for h in range(h0, h0 + hgroup)]
        qs = [q_ref[0, :, sl] for sl in sls]
        carry = tuple((jnp.full((tq, 1), NEG, F32), jnp.zeros((tq, 1), F32), jnp.zeros((tq, HP), F32))
                      for _ in sls)
        for s in range(n_seg):
            k_ref, v_ref = refs[2 * s], refs[2 * s + 1]
            ck = min(chunk, k_ref.shape[1])

            def body(c, carry, k_ref=k_ref, v_ref=v_ref, ck=ck):
                k0 = pl.multiple_of(c * ck, ck)
                out = []
                for (m, l, acc), qh, sl in zip(carry, qs, sls):
                    kc = k_ref[0, pl.ds(k0, ck), sl]
                    vc = v_ref[0, pl.ds(k0, ck), sl]
                    sc = lax.dot_general(qh, kc, (((1,), (1,)), ((), ())), preferred_element_type=F32)
                    m_new = jnp.maximum(m, jnp.max(sc, axis=-1, keepdims=True))
                    alpha = jnp.exp(m - m_new)
                    p = jnp.exp(sc - m_new)
                    l = alpha * l + jnp.sum(p, axis=-1, keepdims=True)
                    acc = alpha * acc + jnp.dot(p.astype(BF16), vc, preferred_element_type=F32)
                    out.append((m_new, l, acc))
                return tuple(out)

            carry = lax.fori_loop(0, k_ref.shape[1] // ck, body, carry)
        for (_, l, acc), sl in zip(carry, sls):
            o_ref[0, :, sl] = acc / l


def _attention(q, kv_segs):
    b, sq, w = q.shape
    tq = 256
    in_specs = [pl.BlockSpec((1, tq, w), lambda bi, i: (bi, i, 0))]
    args = [q]
    for k, v in kv_segs:
        sk = k.shape[1]
        in_specs += [pl.BlockSpec((1, sk, w), lambda bi, i: (bi, 0, 0))] * 2
        args += [k, v]
    return pl.pallas_call(
        functools.partial(_attn_kernel, n_seg=len(kv_segs), chunk=512, hgroup=4),
        grid=(b, sq // tq),
        in_specs=in_specs,
        out_specs=pl.BlockSpec((1, tq, w), lambda bi, i: (bi, i, 0)),
        out_shape=jax.ShapeDtypeStruct((b, sq, w), F32),
        compiler_params=_params(("arbitrary", "arbitrary"), 56),
        name="attention",
    )(*args)


def _dft_mats(n):
    k = np.arange(n)
    ang = 2.0 * np.pi * ((k[:, None] * k[None, :]) % n) / n
    return np.cos(ang).astype(np.float32), np.sin(ang).astype(np.float32)


def _channel_dft_mats():
    c, s = _dft_mats(FNET_GROUP_DIM)
    eye = np.eye(FNET_GROUPS, dtype=np.float32)
    return np.kron(eye, c), np.kron(eye, s)


def _fft1_kernel(x_ref, c_ref, s_ref, tr_ref, ti_ref, br_ref, bi_ref):
    x = x_ref[0]
    ar = jnp.dot(c_ref[...], x, precision=HI, preferred_element_type=F32)
    ai = -jnp.dot(s_ref[...], x, precision=HI, preferred_element_type=F32)
    tr, ti = tr_ref[...], ti_ref[...]
    br_ref[0] = ar * tr - ai * ti
    bi_ref[0] = ar * ti + ai * tr


def _fft2_kernel(br_ref, bi_ref, c_ref, s_ref, cc_ref, sc_ref, o_ref, *, nk, scale):
    c, s = c_ref[...], s_ref[...]
    dot = functools.partial(jnp.dot, precision=HI, preferred_element_type=F32)
    for j in range(nk):
        br, bi = br_ref[0, j], bi_ref[0, j]
        yr = dot(c, br) + dot(s, bi)
        yi = dot(c, bi) - dot(s, br)
        o_ref[0, :, j * FNET_C:(j + 1) * FNET_C] = (dot(yr, cc_ref[...]) + dot(yi, sc_ref[...])) * scale


def _fourier_latent(z):
    b, s, _ = z.shape
    assert s == FFT_N * FFT_N
    n = FFT_N
    cn, sn = _dft_mats(n)
    k1 = np.arange(n)
    ang = 2.0 * np.pi * ((k1[:, None] * k1[None, :]) % s) / s
    tr = jnp.repeat(jnp.asarray(np.cos(ang), F32), FNET_C, axis=1)
    ti = jnp.repeat(jnp.asarray(-np.sin(ang), F32), FNET_C, axis=1)
    wide = n * FNET_C
    tl = 2048
    mat = pl.BlockSpec((n, n), lambda bi, j: (0, 0))
    br, bi = pl.pallas_call(
        _fft1_kernel,
        grid=(b, wide // tl),
        in_specs=[pl.BlockSpec((1, n, tl), lambda bi, j: (bi, 0, j)), mat, mat,
                  pl.BlockSpec((n, tl), lambda bi, j: (0, j)), pl.BlockSpec((n, tl), lambda bi, j: (0, j))],
        out_specs=[pl.BlockSpec((1, n, tl), lambda bi, j: (bi, 0, j))] * 2,
        out_shape=[jax.ShapeDtypeStruct((b, n, wide), F32)] * 2,
        compiler_params=_params(("arbitrary", "arbitrary"), 32),
        name="fft_stage1",
    )(z.reshape(b, n, wide), cn, sn, tr, ti)
    cc, sc = _channel_dft_mats()
    nk = 8
    cmat = pl.BlockSpec((FNET_C, FNET_C), lambda bi, j: (0, 0))
    blk = pl.BlockSpec((1, nk, n, FNET_C), lambda bi, j: (bi, j, 0, 0))
    out = pl.pallas_call(
        functools.partial(_fft2_kernel, nk=nk, scale=float((s * FNET_GROUP_DIM) ** -0.5)),
        grid=(b, n // nk),
        in_specs=[blk, blk, mat, mat, cmat, cmat],
        out_specs=pl.BlockSpec((1, n, nk * FNET_C), lambda bi, j: (bi, 0, j)),
        out_shape=jax.ShapeDtypeStruct((b, n, wide), F32),
        compiler_params=_params(("arbitrary", "arbitrary"), 32),
        name="fft_stage2",
    )(br.reshape(b, n, n, FNET_C), bi.reshape(b, n, n, FNET_C), cn, sn, cc, sc)
    return out.reshape(b, s, FNET_C)


def _dft_direct_kernel(x_ref, c_ref, s_ref, cc_ref, sc_ref, o_ref, *, scale):
    dot = functools.partial(jnp.dot, precision=HI, preferred_element_type=F32)
    x = x_ref[0]
    yr = dot(c_ref[...], x)
    yi = -dot(s_ref[...], x)
    o_ref[0] = (dot(yr, cc_ref[...]) + dot(yi, sc_ref[...])) * scale


def _fourier_direct(z):
    b, s, _ = z.shape
    cn, sn = _dft_mats(s)
    cc, sc = _channel_dft_mats()
    mat = pl.BlockSpec((s, s), lambda bi: (0, 0))
    cmat = pl.BlockSpec((FNET_C, FNET_C), lambda bi: (0, 0))
    blk = pl.BlockSpec((1, s, FNET_C), lambda bi: (bi, 0, 0))
    return pl.pallas_call(
        functools.partial(_dft_direct_kernel, scale=float((s * FNET_GROUP_DIM) ** -0.5)),
        grid=(b,),
        in_specs=[blk, mat, mat, cmat, cmat],
        out_specs=blk,
        out_shape=jax.ShapeDtypeStruct((b, s, FNET_C), F32),
        compiler_params=_params(("arbitrary",), 32),
        name="dft_direct",
    )(z, cn, sn, cc, sc)


def _conv_kernel(z_ref, w_ref, bdw_ref, g_ref, b_ref, o_ref, u_ref, *, rows):
    s = z_ref.shape[1]
    zero = jnp.zeros((CONV_PAD, CONV_C), F32)
    u_ref[0:CONV_PAD, :] = zero
    u_ref[CONV_PAD + s:CONV_PAD + s + CONV_PAD, :] = zero
    for r0 in range(0, s, rows):
        a = z_ref[0, r0:r0 + rows, 0:CONV_C]
        g = z_ref[0, r0:r0 + rows, CONV_C:2 * CONV_C]
        u_ref[CONV_PAD + r0:CONV_PAD + r0 + rows, :] = a * jax.nn.sigmoid(g)
    off = CONV_PAD - CONV_W // 2
    for r0 in range(0, s, rows):
        acc = jnp.zeros((rows, CONV_C), F32) + bdw_ref[...]
        for j in range(CONV_W):
            acc = acc + u_ref[off + r0 + j:off + r0 + j + rows, :] * w_ref[j:j + 1, :]
        mu = jnp.mean(acc, axis=-1, keepdims=True)
        d = acc - mu
        var = jnp.mean(d * d, axis=-1, keepdims=True)
        y = d * lax.rsqrt(var + EPS) * g_ref[...] + b_ref[...]
        o_ref[0, r0:r0 + rows, :] = y * jax.nn.sigmoid(y)


def _conformer_conv(z, lw):
    b, s, _ = z.shape
    vec = pl.BlockSpec((1, CONV_C), lambda bi: (0, 0))
    return pl.pallas_call(
        functools.partial(_conv_kernel, rows=min(256, s)),
        grid=(b,),
        in_specs=[pl.BlockSpec((1, s, 2 * CONV_C), lambda bi: (bi, 0, 0)),
                  pl.BlockSpec((CONV_W + 1, CONV_C), lambda bi: (0, 0)), vec, vec, vec],
        out_specs=pl.BlockSpec((1, s, CONV_C), lambda bi: (bi, 0, 0)),
        out_shape=jax.ShapeDtypeStruct((b, s, CONV_C), F32),
        scratch_shapes=[pltpu.VMEM((s + 2 * CONV_PAD, CONV_C), F32)],
        compiler_params=_params(("arbitrary",), 48),
        name="conformer_conv",
    )(z, lw['w_dw'], lw['b_dw'], lw['g_cln'], lw['b_cln'])


def _merge_kernel(a_ref, f_ref, c_ref, zg_ref, x_ref, mod_ref, g2_ref, wa_ref, wf_ref, wc_ref, wo_ref,
                  xn_ref, h2_ref):
    dot = functools.partial(jnp.dot, preferred_element_type=F32)
    m = jax.nn.sigmoid(zg_ref[0, :, 0:D]) * dot(a_ref[0].astype(BF16), wa_ref[...])
    m = m + jax.nn.sigmoid(zg_ref[0, :, D:2 * D]) * dot(f_ref[0].astype(BF16), wf_ref[...])
    m = m + jax.nn.sigmoid(zg_ref[0, :, 2 * D:3 * D]) * dot(c_ref[0].astype(BF16), wc_ref[...])
    y = dot(m.astype(BF16), wo_ref[...])
    xn = x_ref[0] + mod_ref[0, :, 2 * D:3 * D] * y
    xn_ref[0] = xn
    h2_ref[0] = _rms(xn, D) * g2_ref[...] * (1.0 + mod_ref[0, :, 4 * D:5 * D]) + mod_ref[0, :, 3 * D:4 * D]


def _merge(a_o, fm, cv, zg, x, mods, mod_row, g2, lw):
    b, s, _ = x.shape
    tm = 256
    tok = lambda w: pl.BlockSpec((1, tm, w), lambda bi, i: (bi, i, 0))
    full = lambda r, c: pl.BlockSpec((r, c), lambda bi, i: (0, 0))
    return pl.pallas_call(
        _merge_kernel,
        grid=(b, s // tm),
        in_specs=[tok(HEADS * HP), tok(FNET_C), tok(CONV_C), tok(3 * D), tok(D),
                  pl.BlockSpec((1, 1, 6 * D), lambda bi, i: (mod_row(bi), 0, 0)), full(1, D),
                  full(HEADS * HP, D), full(FNET_C, D), full(CONV_C, D), full(D, D)],
        out_specs=[tok(D), tok(D)],
        out_shape=[jax.ShapeDtypeStruct((b, s, D), F32)] * 2,
        compiler_params=_params(("arbitrary", "arbitrary"), 48),
        name="merge",
    )(a_o, fm, cv, zg, x, mods, g2.reshape(1, D), lw['wb_attn'], lw['wb_fnet'], lw['wb_conv'], lw['w_out'])


def _top_rows(s, k):
    r = s.shape[0]
    rows = lax.broadcasted_iota(I32, s.shape, 0)
    vals, idxs = [], []
    for _ in range(k):
        m = jnp.max(s, axis=0, keepdims=True)
        i = jnp.min(jnp.where(s == m, rows, r), axis=0, keepdims=True)
        vals.append(m)
        idxs.append(i)
        s = jnp.where(rows == i, -jnp.inf, s)
    return jnp.concatenate(vals, axis=0), jnp.concatenate(idxs, axis=0)


def _peer_score_kernel(h_ref, wq_ref, keys_ref, eid_ref, off_ref, gate_ref, qt_ref, ts_ref, ti_ref):
    tn = h_ref.shape[0]
    half = D_KEY // 2
    qt_ref[...] = lax.dot_general(wq_ref[...], h_ref[...], (((1,), (1,)), ((), ())), precision=HI,
                                  preferred_element_type=F32)

    def sub_scores(hp, carry):
        r0 = pl.multiple_of(hp * half, half)
        for lt in range(tn // LANES):
            ls = slice(lt * LANES, (lt + 1) * LANES)
            s = jnp.dot(keys_ref[hp], qt_ref[pl.ds(r0, half), ls], precision=HI, preferred_element_type=F32)
            v, i = _top_rows(s, PEER_TOPK)
            ts_ref[hp, :, ls] = v
            ti_ref[hp, :, ls] = i
        return carry

    lax.fori_loop(0, 2 * PEER_HEADS, sub_scores, 0)

    def head(h, carry):
        for lt in range(tn // LANES):
            ls = slice(lt * LANES, (lt + 1) * LANES)
            s1, s2 = ts_ref[2 * h, :, ls], ts_ref[2 * h + 1, :, ls]
            i1, i2 = ti_ref[2 * h, :, ls], ti_ref[2 * h + 1, :, ls]
            cand = jnp.concatenate([s1[a:a + 1, :] + s2 for a in range(PEER_TOPK)], axis=0)
            cid = jnp.concatenate([i1[a:a + 1, :] * N_KEYS + i2 for a in range(PEER_TOPK)], axis=0)
            best, pos = _top_rows(cand, PEER_TOPK)
            rows = lax.broadcasted_iota(I32, cand.shape, 0)
            eid = jnp.concatenate(
                [jnp.max(jnp.where(rows == pos[a:a + 1, :], cid, -1), axis=0, keepdims=True)
                 for a in range(PEER_TOPK)], axis=0)
            e = jnp.exp(best - best[0:1, :])
            eid_ref[h, :, ls] = eid
            off_ref[h, :, ls] = (eid >> 1) * SUBLANES
            gate_ref[h, :, ls] = e / jnp.sum(e, axis=0, keepdims=True)
        return carry

    lax.fori_loop(0, PEER_HEADS, head, 0)


def _peer_scores(h2, wq_t, keys):
    t = h2.shape[0]
    tn = 256
    out_blk = pl.BlockSpec((PEER_HEADS, PEER_TOPK, tn), lambda i: (0, 0, i))
    return pl.pallas_call(
        _peer_score_kernel,
        grid=(t // tn,),
        in_specs=[pl.BlockSpec((tn, D), lambda i: (i, 0)),
                  pl.BlockSpec((PEER_HEADS * D_KEY, D), lambda i: (0, 0)),
                  pl.BlockSpec((2 * PEER_HEADS, N_KEYS, D_KEY // 2), lambda i: (0, 0, 0))],
        out_specs=[out_blk, out_blk, out_blk],
        out_shape=[jax.ShapeDtypeStruct((PEER_HEADS, PEER_TOPK, t), I32),
                   jax.ShapeDtypeStruct((PEER_HEADS, PEER_TOPK, t), I32),
                   jax.ShapeDtypeStruct((PEER_HEADS, PEER_TOPK, t), F32)],
        scratch_shapes=[pltpu.VMEM((PEER_HEADS * D_KEY, tn), F32),
                        pltpu.VMEM((2 * PEER_HEADS, PEER_TOPK, tn), F32),
                        pltpu.VMEM((2 * PEER_HEADS, PEER_TOPK, tn), I32)],
        compiler_params=_params(("arbitrary",), 32),
        name="peer_scores",
    )(h2, wq_t, keys)


PEER_TN = 128
STAGE_ROWS = PEER_SEL * 2 * SUBLANES
NT_DIMS = (((1,), (1,)), ((), ()))


def _stage_layout():
    s = lax.broadcasted_iota(I32, (SUBLANES, LANES), 0)
    lo = lax.bitcast_convert_type((s + 1).astype(F32), U32) >> 16
    hi = lax.bitcast_convert_type((s + 1 + SUBLANES).astype(F32), U32) & jnp.uint32(0xFFFF0000)
    probe = pltpu.bitcast(hi | lo, BF16)
    code16 = lax.dot_general(jnp.ones((SUBLANES, LANES), BF16), probe, NT_DIMS,
                             preferred_element_type=F32) * (1.0 / LANES)
    r = lax.broadcasted_iota(I32, (2 * SUBLANES, STAGE_ROWS), 0)
    q = lax.broadcasted_iota(I32, (2 * SUBLANES, STAGE_ROWS), 1)
    tile16 = ((q & (2 * SUBLANES - 1)) == r).astype(F32)
    code = jnp.dot(code16, tile16, precision=HI, preferred_element_type=F32).astype(I32) - 1
    src_sub = code & (SUBLANES - 1)
    chunk = (src_sub & 3) + 4 * (code >> 3)
    return chunk, src_sub >> 2


def _layout_tables(kmask_ref, e8_ref):
    chunk, pair = _stage_layout()
    sub = lax.broadcasted_iota(I32, (SUBLANES, STAGE_ROWS), 0)
    kmask_ref[...] = (sub == chunk).astype(F32)
    a = lax.broadcasted_iota(I32, (2 * PEER_SEL, STAGE_ROWS), 0)
    q = lax.broadcasted_iota(I32, (2 * PEER_SEL, STAGE_ROWS), 1)
    e8_ref[...] = (((q >> 4) == (a >> 1)) & (pair[0:1, :] == (a & 1))).astype(BF16)


def _split3(x):
    x1 = x.astype(BF16)
    r1 = x - x1.astype(F32)
    x2 = r1.astype(BF16)
    return x1, x2, (r1 - x2.astype(F32)).astype(BF16)


def _stage_token(t, off_smem, tab_vmem, g_ref):
    for j in range(PEER_SEL):
        off = pl.multiple_of(off_smem[t, j], SUBLANES)
        g_ref[j * SUBLANES:(j + 1) * SUBLANES, :] = tab_vmem[pl.ds(off, SUBLANES), :]


def _staged(g_ref):
    return pltpu.bitcast(g_ref[...], BF16)


def _token_pairs(stage, compute):
    stage(0, 0)

    def pair(k, carry):
        t = 2 * k
        stage(t + 1, 1)
        compute(t, 0)
        stage(jnp.minimum(t + 2, PEER_TN - 1), 0)
        compute(t + 1, 1)
        return carry

    lax.fori_loop(0, PEER_TN // 2, pair, 0)


def _sublane_allsum(p):
    p = p + pltpu.roll(p, 4, 0)
    p = p + pltpu.roll(p, 2, 0)
    return p + pltpu.roll(p, 1, 0)


def _load_tile_indices(off_hbm, off_smem, sem):
    i = pl.program_id(0)
    cp = pltpu.make_async_copy(off_hbm.at[pl.ds(i * PEER_TN, PEER_TN)], off_smem, sem)
    cp.start()
    cp.wait()


def _load_table_once(tab_hbm, tab_vmem, sem):
    @pl.when(pl.program_id(0) == 0)
    def _():
        cp = pltpu.make_async_copy(tab_hbm, tab_vmem, sem)
        cp.start()
        cp.wait()


def _gather_scratch(table_shape):
    return [pltpu.VMEM(table_shape, U32), pltpu.SMEM((PEER_TN, PEER_SEL), I32),
            pltpu.VMEM((PEER_SEL * SUBLANES, LANES), U32), pltpu.VMEM((PEER_SEL * SUBLANES, LANES), U32),
            pltpu.VMEM((SUBLANES, STAGE_ROWS), F32), pltpu.VMEM((2 * PEER_SEL, STAGE_ROWS), BF16),
            pltpu.SemaphoreType.DMA((2,))]


def _gather_prologue(off_hbm, tab_hbm, tab_vmem, off_smem, kmask_ref, e8_ref, sems):
    _load_table_once(tab_hbm, tab_vmem, sems.at[0])
    _load_tile_indices(off_hbm, off_smem, sems.at[1])

    @pl.when(pl.program_id(0) == 0)
    def _():
        _layout_tables(kmask_ref, e8_ref)


def _peer_u_kernel(off_hbm, tab_hbm, h3_ref, eid_ref, gate_ref, w2_ref,
                   tab_vmem, off_smem, g0_ref, g1_ref, kmask_ref, e8_ref, sems, s_ref):
    _gather_prologue(off_hbm, tab_hbm, tab_vmem, off_smem, kmask_ref, e8_ref, sems)
    bufs = (g0_ref, g1_ref)
    sub = lax.broadcasted_iota(I32, (SUBLANES, STAGE_ROWS), 0)
    s_ref[...] = jnp.zeros(s_ref.shape, F32)

    def compute(t, b):
        h = h3_ref[t]
        h_hi = h.astype(BF16)
        h_lo = (h - h_hi.astype(F32)).astype(BF16)
        r = lax.dot_general(jnp.concatenate([h_hi, h_lo], axis=0), _staged(bufs[b]), NT_DIMS,
                            preferred_element_type=F32)
        tot = _sublane_allsum((r[0:SUBLANES] + r[SUBLANES:]) * kmask_ref[...])
        t8 = pl.multiple_of((t >> 3) << 3, SUBLANES)
        s_ref[pl.ds(t8, SUBLANES), :] = jnp.where(sub == (t & 7), tot, s_ref[pl.ds(t8, SUBLANES), :])

    _token_pairs(lambda t, b: _stage_token(t, off_smem, tab_vmem, bufs[b]), compute)
    e8 = e8_ref[...]
    act2 = sum(lax.dot_general(sk, e8, NT_DIMS, preferred_element_type=F32) for sk in _split3(s_ref[...]))
    j = lax.broadcasted_iota(I32, (PEER_SEL, 2 * PEER_SEL), 0)
    a = lax.broadcasted_iota(I32, (PEER_SEL, 2 * PEER_SEL), 1)
    dup = ((a >> 1) == j).astype(BF16)
    gate2 = sum(jnp.dot(gk, dup, preferred_element_type=F32) for gk in _split3(gate_ref[...]))
    pair2 = jnp.dot((eid_ref[...] & 1).astype(BF16), dup, preferred_element_type=F32)
    slot_pair = (lax.broadcasted_iota(I32, act2.shape, 1) & 1).astype(F32)
    w2_ref[...] = jnp.where(pair2 == slot_pair, jax.nn.gelu(act2) * gate2, 0.0)


def _peer_u(off_tok, eid_tok, gate_tok, u_packed, h3):
    t = off_tok.shape[0]
    tok = lambda w: pl.BlockSpec((PEER_TN, w), lambda i: (i, 0))
    any_spec = pl.BlockSpec(memory_space=pl.ANY)
    return pl.pallas_call(
        _peer_u_kernel,
        grid=(t // PEER_TN,),
        in_specs=[any_spec, any_spec, pl.BlockSpec((PEER_TN, SUBLANES, LANES), lambda i: (i, 0, 0)),
                  tok(PEER_SEL), tok(PEER_SEL)],
        out_specs=tok(2 * PEER_SEL),
        out_shape=jax.ShapeDtypeStruct((t, 2 * PEER_SEL), F32),
        scratch_shapes=_gather_scratch(u_packed.shape) + [pltpu.VMEM((PEER_TN, STAGE_ROWS), F32)],
        compiler_params=_params(("arbitrary",), 52),
        name="peer_u",
    )(off_tok, u_packed, h3, eid_tok, gate_tok)


def _peer_v_kernel(off_hbm, tab_hbm, w2_ref, x3_ref, gt_ref, o_ref,
                   tab_vmem, off_smem, g0_ref, g1_ref, kmask_ref, e8_ref, sems, wx_ref):
    _gather_prologue(off_hbm, tab_hbm, tab_vmem, off_smem, kmask_ref, e8_ref, sems)
    bufs = (g0_ref, g1_ref)
    w2 = w2_ref[...]
    w_hi = w2.astype(BF16)
    w_lo = (w2 - w_hi.astype(F32)).astype(BF16)
    e8 = e8_ref[...]
    wx_ref[0] = jnp.dot(w_hi, e8, preferred_element_type=F32)
    wx_ref[1] = jnp.dot(w_lo, e8, preferred_element_type=F32)
    sub = lax.broadcasted_iota(I32, (SUBLANES, STAGE_ROWS), 0)

    def compute(t, b):
        t8 = pl.multiple_of((t >> 3) << 3, SUBLANES)
        pick = sub == (t & 7)
        kmask = kmask_ref[...]
        rows = []
        for k in range(2):
            w = _sublane_allsum(jnp.where(pick, wx_ref[k, pl.ds(t8, SUBLANES), :], 0.0))
            rows.append((w * kmask).astype(BF16))
        o = jnp.dot(jnp.concatenate(rows, axis=0), _staged(bufs[b]), preferred_element_type=F32)
        o_ref[t] = x3_ref[t] + gt_ref[0] * (o[0:SUBLANES] + o[SUBLANES:])

    _token_pairs(lambda t, b: _stage_token(t, off_smem, tab_vmem, bufs[b]), compute)


def _peer_v(off_tok, w2, v_packed, x3, gt3, gate_row):
    t = off_tok.shape[0]
    any_spec = pl.BlockSpec(memory_space=pl.ANY)
    tok = pl.BlockSpec((PEER_TN, SUBLANES, LANES), lambda i: (i, 0, 0))
    return pl.pallas_call(
        _peer_v_kernel,
        grid=(t // PEER_TN,),
        in_specs=[any_spec, any_spec, pl.BlockSpec((PEER_TN, 2 * PEER_SEL), lambda i: (i, 0)), tok,
                  pl.BlockSpec((1, SUBLANES, LANES), lambda i: (gate_row(i), 0, 0))],
        out_specs=tok,
        out_shape=jax.ShapeDtypeStruct(x3.shape, F32),
        scratch_shapes=_gather_scratch(v_packed.shape) + [pltpu.VMEM((2, PEER_TN, STAGE_ROWS), F32)],
        compiler_params=_params(("arbitrary",), 52),
        name="peer_v",
    )(off_tok, v_packed, w2, x3, gt3)


def _pack_table(tab):
    n = tab.shape[0]
    bits = lax.bitcast_convert_type(tab.astype(BF16), jnp.uint16).astype(U32)
    words = bits[:, :D // 2] | (bits[:, D // 2:] << 16)
    return words.reshape(n * D // (2 * LANES), LANES)


def _peer(xn, h2, gt3, gate_row, pw):
    b, s, _ = xn.shape
    t = b * s
    eid_sel, off_sel, gate_sel = _peer_scores(h2.reshape(t, D), pw['wq_t'], pw['keys'])
    eid_tok, off_tok, gate_tok = (a.reshape(PEER_SEL, t).T for a in (eid_sel, off_sel, gate_sel))
    w2 = _peer_u(off_tok, eid_tok, gate_tok, pw['u_packed'], h2.reshape(t, SUBLANES, LANES))
    out = _peer_v(off_tok, w2, pw['v_packed'], xn.reshape(t, SUBLANES, LANES), gt3,
                  functools.partial(gate_row, tiles_per_seq=s // PEER_TN))
    return out.reshape(b, s, D)


def _rope_tables(s):
    pos = np.arange(s)
    half = QK_ROPE // 2
    inv = ROPE_BASE ** (-np.arange(0, half, 2, dtype=np.float64) / half)
    ar = (pos // GRID_W)[:, None] * inv
    ac = (pos % GRID_W)[:, None] * inv
    ang = np.concatenate([ar, ar, ac, ac], axis=-1)
    cos = np.ones((s, HP), np.float32)
    sin_a = np.zeros((s, HP), np.float32)
    sin_b = np.zeros((s, HP), np.float32)
    cos[:, QK_NOPE:QK_HEAD] = np.cos(ang)
    sin = np.sin(ang)
    first = (np.arange(QK_ROPE) % (half)) < half // 2
    sin_a[:, QK_NOPE:QK_HEAD] = np.where(first, -sin, 0.0)
    sin_b[:, QK_NOPE:QK_HEAD] = np.where(first, 0.0, sin)
    return jnp.asarray(cos), jnp.asarray(sin_a), jnp.asarray(sin_b)


def _identity_tables(s):
    return jnp.ones((s, HP), F32), jnp.zeros((s, HP), F32), jnp.zeros((s, HP), F32)


def _pad_heads(w, width):
    k = w.shape[0]
    w = w.reshape(k, HEADS, width)
    return jnp.pad(w, ((0, 0), (0, 0), (0, HP - width))).reshape(k, HEADS * HP)


def _pad_lanes(g):
    return jnp.pad(g, (0, HP - g.shape[0])).reshape(1, HP)


def _layer_weights(i, w_in, g_ckv, w_ukv, g_cq, w_uq, g_qn, g_kn, w_dw, b_dw, g_cln, b_cln,
                   wb_attn, wb_fnet, wb_conv, w_out):
    win = w_in[i]
    w_pad = jnp.concatenate([win[:, :KV_IN], jnp.zeros((D, KV_IN_PAD - KV_IN), F32), win[:, KV_IN:]], axis=1)
    ukv = w_ukv[i].reshape(KV_LORA, HEADS, QK_NOPE + V_HEAD)
    wb_a = wb_attn[i].reshape(HEADS, V_HEAD, D)
    return {
        'w_in': w_pad.astype(BF16),
        'g_ckv': g_ckv[i].reshape(1, KV_LORA), 'g_cq': g_cq[i].reshape(1, Q_LORA),
        'g_qn': _pad_lanes(g_qn[i]), 'g_kn': _pad_lanes(g_kn[i]),
        'w_uk': _pad_heads(ukv[:, :, :QK_NOPE].reshape(KV_LORA, HEADS * QK_NOPE), QK_NOPE).astype(BF16),
        'w_uv': _pad_heads(ukv[:, :, QK_NOPE:].reshape(KV_LORA, HEADS * V_HEAD), V_HEAD).astype(BF16),
        'w_uq': _pad_heads(w_uq[i], QK_HEAD).astype(BF16),
        'w_dw': jnp.pad(w_dw[i].reshape(CONV_W, CONV_C), ((0, 1), (0, 0))),
        'b_dw': b_dw[i].reshape(1, CONV_C), 'g_cln': g_cln[i].reshape(1, CONV_C),
        'b_cln': b_cln[i].reshape(1, CONV_C),
        'wb_attn': jnp.pad(wb_a, ((0, 0), (0, HP - V_HEAD), (0, 0))).reshape(HEADS * HP, D).astype(BF16),
        'wb_fnet': wb_fnet[i].astype(BF16), 'wb_conv': wb_conv[i].astype(BF16), 'w_out': w_out[i].astype(BF16),
    }


def _mixer_branches(zf, zc, lw):
    fm = _fourier_latent(zf) if zf.shape[1] == FFT_N * FFT_N else _fourier_direct(zf)
    return fm, _conformer_conv(zc, lw)


def kernel(x, c, ctx, c_ctx, w_mod, b_mod, g_norm1, w_in, g_ckv, w_ukv, g_cq, w_uq, g_qn, g_kn, w_dw, b_dw,
           g_cln, b_cln, wb_attn, wb_fnet, wb_conv, w_out, g_norm2, w_query, sub_keys, u_tab, v_tab):
    b, s, _ = x.shape
    n_ctx = ctx.shape[1]
    depth = w_mod.shape[0]
    ctx_row = b
    cond = jnp.concatenate([c, c_ctx[None, :], jnp.zeros((16 - b - 1, D), F32)], axis=0)
    mods_all = _adaln(cond, w_mod, b_mod)
    rope = _rope_tables(s)
    no_rope = _identity_tables(n_ctx)
    x_row = lambda bi: bi
    c_row = lambda bi: ctx_row
    x_gate_row = lambda i, tiles_per_seq: i // tiles_per_seq
    c_gate_row = lambda i, tiles_per_seq: ctx_row
    pos_tiles = s // 256
    for i in range(depth):
        lw = _layer_weights(i, w_in, g_ckv, w_ukv, g_cq, w_uq, g_qn, g_kn, w_dw, b_dw, g_cln, b_cln,
                            wb_attn, wb_fnet, wb_conv, w_out)
        pw = {'wq_t': w_query[i].T, 'keys': sub_keys[i].reshape(2 * PEER_HEADS, N_KEYS, D_KEY // 2),
              'u_packed': _pack_table(u_tab[i]), 'v_packed': _pack_table(v_tab[i])}
        mods = mods_all[i].reshape(16, 1, 6 * D)
        gt3 = mods_all[i][:, 5 * D:6 * D].reshape(16, SUBLANES, LANES)
        update_ctx = i < depth - 1

        kv_x, cq_x, zf_x, zc_x, zg_x = _inproj(x, mods, x_row, g_norm1[i], lw['w_in'], IN_SEGS)
        if update_ctx:
            kv_c, cq_c, zf_c, zc_c, zg_c = _inproj(ctx, mods, c_row, g_norm1[i], lw['w_in'], IN_SEGS)
        else:
            kv_c, cq_c = _inproj(ctx, mods, c_row, g_norm1[i], lw['w_in'][:, :KV_IN_PAD + Q_LORA], IN_SEGS[:2])
        q_c, k_c, v_c = _mla_prep(kv_c, cq_c, no_rope, lambda t: 0, lw)
        q_x, k_x, v_x = _mla_prep(kv_x, cq_x, rope, lambda t: t % pos_tiles, lw)

        a_x = _attention(q_x, [(k_c, v_c), (k_x, v_x)])
        fm_x, cv_x = _mixer_branches(zf_x, zc_x, lw)
        xn, h2 = _merge(a_x, fm_x, cv_x, zg_x, x, mods, x_row, g_norm2[i], lw)
        x = _peer(xn, h2, gt3, x_gate_row, pw)
        if update_ctx:
            a_c = _attention(q_c, [(k_c, v_c)])
            fm_c, cv_c = _mixer_branches(zf_c, zc_c, lw)
            cn, ch2 = _merge(a_c, fm_c, cv_c, zg_c, ctx, mods, c_row, g_norm2[i], lw)
            ctx = _peer(cn, ch2, gt3, c_gate_row, pw)
    return x
```

```python
import functools

import numpy as np
import jax
import jax.numpy as jnp
from jax import lax
from jax.experimental import pallas as pl
from jax.experimental.pallas import tpu as pltpu

F32 = jnp.float32
BF16 = jnp.bfloat16
I32 = jnp.int32
U32 = jnp.uint32
HI = lax.Precision.HIGHEST

LANES = 128
SUBLANES = 8
VMEM_BYTES_V7X = 64 * 1024 * 1024

D = 1024
GRID_W = 64
EPS = 1e-6
HEADS = 8
Q_LORA = 256
KV_LORA = 128
QK_NOPE = 64
QK_ROPE = 32
V_HEAD = 64
QK_HEAD = QK_NOPE + QK_ROPE
ROPE_BASE = 10000.0
HP = LANES
FNET_GROUPS = 4
FNET_GROUP_DIM = 64
FNET_C = FNET_GROUPS * FNET_GROUP_DIM
CONV_C = 256
CONV_W = 31
CONV_PAD = 16
PEER_HEADS = 8
N_KEYS = 128
D_KEY = 128
PEER_TOPK = 16
PEER_SEL = PEER_HEADS * PEER_TOPK
KV_IN = KV_LORA + QK_ROPE
KV_IN_PAD = 2 * LANES
IN_SEGS = (KV_IN_PAD, Q_LORA, FNET_C, 2 * CONV_C, 3 * D)
FFT_N = 64
NEG = -1e30


def _params(sem, vmem_mb):
    return pltpu.CompilerParams(dimension_semantics=sem, vmem_limit_bytes=vmem_mb * 1024 * 1024)


def _rms(x, n):
    return x * lax.rsqrt(jnp.sum(x * x, axis=-1, keepdims=True) * (1.0 / n) + EPS)


def _adaln_kernel(c_ref, w_ref, b_ref, o_ref):
    c = c_ref[...]
    s = c * jax.nn.sigmoid(c)
    o_ref[0] = jnp.dot(s, w_ref[0], precision=HI, preferred_element_type=F32) + b_ref[0]


def _adaln(cond, w_mod, b_mod):
    depth = w_mod.shape[0]
    tn = 512
    return pl.pallas_call(
        _adaln_kernel,
        grid=(depth, 6 * D // tn),
        in_specs=[
            pl.BlockSpec((16, D), lambda l, j: (0, 0)),
            pl.BlockSpec((1, D, tn), lambda l, j: (l, 0, j)),
            pl.BlockSpec((1, 1, tn), lambda l, j: (l, 0, j)),
        ],
        out_specs=pl.BlockSpec((1, 16, tn), lambda l, j: (l, 0, j)),
        out_shape=jax.ShapeDtypeStruct((depth, 16, 6 * D), F32),
        compiler_params=_params(("arbitrary", "arbitrary"), 32),
        name="adaln",
    )(cond, w_mod, b_mod.reshape(depth, 1, 6 * D))


def _inproj_kernel(x_ref, mod_ref, g_ref, w_ref, *out_refs, widths):
    x = x_ref[0]
    y = _rms(x, D) * g_ref[...]
    h = (y * (1.0 + mod_ref[0, :, D:2 * D]) + mod_ref[0, :, 0:D]).astype(BF16)
    col = 0
    for o_ref, width in zip(out_refs, widths):
        for c0 in range(0, width, D):
            c1 = min(c0 + D, width)
            o_ref[0, :, c0:c1] = jnp.dot(h, w_ref[:, col + c0:col + c1], preferred_element_type=F32)
        col += width


def _inproj(x, mods, mod_row, g, w_pad, widths):
    b, s, _ = x.shape
    tm = 256
    ncol = sum(widths)
    return pl.pallas_call(
        functools.partial(_inproj_kernel, widths=widths),
        grid=(b, s // tm),
        in_specs=[
            pl.BlockSpec((1, tm, D), lambda bi, i: (bi, i, 0)),
            pl.BlockSpec((1, 1, 6 * D), lambda bi, i: (mod_row(bi), 0, 0)),
            pl.BlockSpec((1, D), lambda bi, i: (0, 0)),
            pl.BlockSpec((D, ncol), lambda bi, i: (0, 0)),
        ],
        out_specs=[pl.BlockSpec((1, tm, w), lambda bi, i: (bi, i, 0)) for w in widths],
        out_shape=[jax.ShapeDtypeStruct((b, s, w), F32) for w in widths],
        compiler_params=_params(("arbitrary", "arbitrary"), 48),
        name="inproj",
    )(x, mods, g.reshape(1, D), w_pad)


def _rope(x, cos, sin_a, sin_b):
    return x * cos + pltpu.roll(x, LANES - 8, 1) * sin_a + pltpu.roll(x, 8, 1) * sin_b


def _mla_prep_kernel(kv_ref, cq_ref, cos_ref, sa_ref, sb_ref, gckv_ref, gcq_ref, gqn_ref, gkn_ref,
                     wuk_ref, wuv_ref, wuq_ref, q_ref, k_ref, v_ref):
    kvx = kv_ref[0]
    ckv = (_rms(kvx[:, :KV_LORA], KV_LORA) * gckv_ref[...]).astype(BF16)
    k_nope = jnp.dot(ckv, wuk_ref[...], preferred_element_type=F32)
    lane = lax.broadcasted_iota(I32, (1, HEADS * HP), 1)
    ones_lane = ((lane & (HP - 1)) == V_HEAD).astype(F32)
    v_ref[0] = (jnp.dot(ckv, wuv_ref[...], preferred_element_type=F32) + ones_lane).astype(BF16)
    k_rope = pltpu.roll(kvx[:, KV_LORA:], QK_NOPE, 1)
    cq = (_rms(cq_ref[0], Q_LORA) * gcq_ref[...]).astype(BF16)
    q = jnp.dot(cq, wuq_ref[...], preferred_element_type=F32)
    cos, sin_a, sin_b = cos_ref[...], sa_ref[...], sb_ref[...]
    for h in range(HEADS):
        sl = slice(h * HP, (h + 1) * HP)
        kh = k_nope[:, sl] + k_rope
        kh = _rms(kh, QK_HEAD) * gkn_ref[...]
        k_ref[0, :, sl] = _rope(kh, cos, sin_a, sin_b).astype(BF16)
        qh = _rms(q[:, sl], QK_HEAD) * gqn_ref[...]
        q_ref[0, :, sl] = (_rope(qh, cos, sin_a, sin_b) * QK_HEAD ** -0.5).astype(BF16)


def _mla_prep(kv, cq, tables, pos_block, lw):
    b, s, _ = kv.shape
    tm = 256
    full = lambda shape: pl.BlockSpec(shape, lambda bi, i: (0,) * len(shape))
    tok = lambda w: pl.BlockSpec((1, tm, w), lambda bi, i: (bi, i, 0))
    tab = pl.BlockSpec((tm, HP), lambda bi, i: (pos_block(i), 0))
    return pl.pallas_call(
        _mla_prep_kernel,
        grid=(b, s // tm),
        in_specs=[tok(KV_IN_PAD), tok(Q_LORA), tab, tab, tab,
                  full((1, KV_LORA)), full((1, Q_LORA)), full((1, HP)), full((1, HP)),
                  full((KV_LORA, HEADS * HP)), full((KV_LORA, HEADS * HP)), full((Q_LORA, HEADS * HP))],
        out_specs=[tok(HEADS * HP)] * 3,
        out_shape=[jax.ShapeDtypeStruct((b, s, HEADS * HP), BF16)] * 3,
        compiler_params=_params(("arbitrary", "arbitrary"), 40),
        name="mla_prep",
    )(kv, cq, *tables, lw['g_ckv'], lw['g_cq'], lw['g_qn'], lw['g_kn'], lw['w_uk'], lw['w_uv'], lw['w_uq'])


def _attn_kernel(q_ref, *refs, n_seg, chunk, hgroup):
    o_ref = refs[2 * n_seg]
    tq = q_ref.shape[1]
    for h0 in range(0, HEADS, hgroup):
        sls = [slice(h * HP, (h + 1) * HP) for h in range(h0, h0 + hgroup)]
        qs = [q_ref[0, :, sl] for sl in sls]
        carry = tuple((jnp.full((tq, 1), NEG, F32), jnp.zeros((tq, HP), F32)) for _ in sls)
        for s in range(n_seg):
            k_ref, v_ref = refs[2 * s], refs[2 * s + 1]
            ck = min(chunk, k_ref.shape[1])

            def body(c, carry, k_ref=k_ref, v_ref=v_ref, ck=ck):
                k0 = pl.multiple_of(c * ck, ck)
                out = []
                for (m, acc), qh, sl in zip(carry, qs, sls):
                    kc = k_ref[0, pl.ds(k0, ck), sl]
                    vc = v_ref[0, pl.ds(k0, ck), sl]
                    sc = lax.dot_general(qh, kc, (((1,), (1,)), ((), ())), preferred_element_type=F32)
                    m_new = jnp.maximum(m, jnp.max(sc, axis=-1, keepdims=True))
                    p = jnp.exp(sc - m_new).astype(BF16)
                    acc = jnp.exp(m - m_new) * acc + jnp.dot(p, vc, preferred_element_type=F32)
                    out.append((m_new, acc))
                return tuple(out)

            carry = lax.fori_loop(0, k_ref.shape[1] // ck, body, carry)
        for (_, acc), sl in zip(carry, sls):
            o_ref[0, :, sl] = acc / acc[:, V_HEAD:V_HEAD + 1]


def _attention(q, kv_segs):
    b, sq, w = q.shape
    tq = 256
    in_specs = [pl.BlockSpec((1, tq, w), lambda bi, i: (bi, i, 0))]
    args = [q]
    for k, v in kv_segs:
        sk = k.shape[1]
        in_specs += [pl.BlockSpec((1, sk, w), lambda bi, i: (bi, 0, 0))] * 2
        args += [k, v]
    return pl.pallas_call(
        functools.partial(_attn_kernel, n_seg=len(kv_segs), chunk=512, hgroup=4),
        grid=(b, sq // tq),
        in_specs=in_specs,
        out_specs=pl.BlockSpec((1, tq, w), lambda bi, i: (bi, i, 0)),
        out_shape=jax.ShapeDtypeStruct((b, sq, w), F32),
        compiler_params=_params(("arbitrary", "arbitrary"), 56),
        name="attention",
    )(*args)


def _dft_mats(n):
    k = np.arange(n)
    ang = 2.0 * np.pi * ((k[:, None] * k[None, :]) % n) / n
    return np.cos(ang).astype(np.float32), np.sin(ang).astype(np.float32)


def _channel_dft_mats():
    c, s = _dft_mats(FNET_GROUP_DIM)
    eye = np.eye(FNET_GROUPS, dtype=np.float32)
    return np.kron(eye, c), np.kron(eye, s)


def _fft1_kernel(x_ref, c_ref, s_ref, tr_ref, ti_ref, br_ref, bi_ref):
    x = x_ref[0]
    ar = jnp.dot(c_ref[...], x, precision=HI, preferred_element_type=F32)
    ai = -jnp.dot(s_ref[...], x, precision=HI, preferred_element_type=F32)
    tr, ti = tr_ref[...], ti_ref[...]
    br_ref[0] = ar * tr - ai * ti
    bi_ref[0] = ar * ti + ai * tr


def _fft2_kernel(br_ref, bi_ref, c_ref, s_ref, cc_ref, sc_ref, o_ref, *, nk, scale):
    c, s = c_ref[...], s_ref[...]
    dot = functools.partial(jnp.dot, precision=HI, preferred_element_type=F32)
    for j in range(nk):
        br, bi = br_ref[0, j], bi_ref[0, j]
        yr = dot(c, br) + dot(s, bi)
        yi = dot(c, bi) - dot(s, br)
        o_ref[0, :, j * FNET_C:(j + 1) * FNET_C] = (dot(yr, cc_ref[...]) + dot(yi, sc_ref[...])) * scale


def _fourier_latent(z):
    b, s, _ = z.shape
    assert s == FFT_N * FFT_N
    n = FFT_N
    cn, sn = _dft_mats(n)
    k1 = np.arange(n)
    ang = 2.0 * np.pi * ((k1[:, None] * k1[None, :]) % s) / s
    tr = jnp.repeat(jnp.asarray(np.cos(ang), F32), FNET_C, axis=1)
    ti = jnp.repeat(jnp.asarray(-np.sin(ang), F32), FNET_C, axis=1)
    wide = n * FNET_C
    tl = 2048
    mat = pl.BlockSpec((n, n), lambda bi, j: (0, 0))
    br, bi = pl.pallas_call(
        _fft1_kernel,
        grid=(b, wide // tl),
        in_specs=[pl.BlockSpec((1, n, tl), lambda bi, j: (bi, 0, j)), mat, mat,
                  pl.BlockSpec((n, tl), lambda bi, j: (0, j)), pl.BlockSpec((n, tl), lambda bi, j: (0, j))],
        out_specs=[pl.BlockSpec((1, n, tl), lambda bi, j: (bi, 0, j))] * 2,
        out_shape=[jax.ShapeDtypeStruct((b, n, wide), F32)] * 2,
        compiler_params=_params(("arbitrary", "arbitrary"), 32),
        name="fft_stage1",
    )(z.reshape(b, n, wide), cn, sn, tr, ti)
    cc, sc = _channel_dft_mats()
    nk = 8
    cmat = pl.BlockSpec((FNET_C, FNET_C), lambda bi, j: (0, 0))
    blk = pl.BlockSpec((1, nk, n, FNET_C), lambda bi, j: (bi, j, 0, 0))
    out = pl.pallas_call(
        functools.partial(_fft2_kernel, nk=nk, scale=float((s * FNET_GROUP_DIM) ** -0.5)),
        grid=(b, n // nk),
        in_specs=[blk, blk, mat, mat, cmat, cmat],
        out_specs=pl.BlockSpec((1, n, nk * FNET_C), lambda bi, j: (bi, 0, j)),
        out_shape=jax.ShapeDtypeStruct((b, n, wide), F32),
        compiler_params=_params(("arbitrary", "arbitrary"), 32),
        name="fft_stage2",
    )(br.reshape(b, n, n, FNET_C), bi.reshape(b, n, n, FNET_C), cn, sn, cc, sc)
    return out.reshape(b, s, FNET_C)


def _dft_direct_kernel(x_ref, c_ref, s_ref, cc_ref, sc_ref, o_ref, *, scale):
    dot = functools.partial(jnp.dot, precision=HI, preferred_element_type=F32)
    x = x_ref[0]
    yr = dot(c_ref[...], x)
    yi = -dot(s_ref[...], x)
    o_ref[0] = (dot(yr, cc_ref[...]) + dot(yi, sc_ref[...])) * scale


def _fourier_direct(z):
    b, s, _ = z.shape
    cn, sn = _dft_mats(s)
    cc, sc = _channel_dft_mats()
    mat = pl.BlockSpec((s, s), lambda bi: (0, 0))
    cmat = pl.BlockSpec((FNET_C, FNET_C), lambda bi: (0, 0))
    blk = pl.BlockSpec((1, s, FNET_C), lambda bi: (bi, 0, 0))
    return pl.pallas_call(
        functools.partial(_dft_direct_kernel, scale=float((s * FNET_GROUP_DIM) ** -0.5)),
        grid=(b,),
        in_specs=[blk, mat, mat, cmat, cmat],
        out_specs=blk,
        out_shape=jax.ShapeDtypeStruct((b, s, FNET_C), F32),
        compiler_params=_params(("arbitrary",), 32),
        name="dft_direct",
    )(z, cn, sn, cc, sc)


def _conv_kernel(z_ref, w_ref, bdw_ref, g_ref, b_ref, o_ref, u_ref, *, rows):
    s = z_ref.shape[1]
    zero = jnp.zeros((CONV_PAD, CONV_C), F32)
    u_ref[0:CONV_PAD, :] = zero
    u_ref[CONV_PAD + s:CONV_PAD + s + CONV_PAD, :] = zero
    for r0 in range(0, s, rows):
        a = z_ref[0, r0:r0 + rows, 0:CONV_C]
        g = z_ref[0, r0:r0 + rows, CONV_C:2 * CONV_C]
        u_ref[CONV_PAD + r0:CONV_PAD + r0 + rows, :] = a * jax.nn.sigmoid(g)
    off = CONV_PAD - CONV_W // 2
    for r0 in range(0, s, rows):
        acc = jnp.zeros((rows, CONV_C), F32) + bdw_ref[...]
        for j in range(CONV_W):
            acc = acc + u_ref[off + r0 + j:off + r0 + j + rows, :] * w_ref[j:j + 1, :]
        mu = jnp.mean(acc, axis=-1, keepdims=True)
        d = acc - mu
        var = jnp.mean(d * d, axis=-1, keepdims=True)
        y = d * lax.rsqrt(var + EPS) * g_ref[...] + b_ref[...]
        o_ref[0, r0:r0 + rows, :] = y * jax.nn.sigmoid(y)


def _conformer_conv(z, lw):
    b, s, _ = z.shape
    vec = pl.BlockSpec((1, CONV_C), lambda bi: (0, 0))
    return pl.pallas_call(
        functools.partial(_conv_kernel, rows=min(256, s)),
        grid=(b,),
        in_specs=[pl.BlockSpec((1, s, 2 * CONV_C), lambda bi: (bi, 0, 0)),
                  pl.BlockSpec((CONV_W + 1, CONV_C), lambda bi: (0, 0)), vec, vec, vec],
        out_specs=pl.BlockSpec((1, s, CONV_C), lambda bi: (bi, 0, 0)),
        out_shape=jax.ShapeDtypeStruct((b, s, CONV_C), F32),
        scratch_shapes=[pltpu.VMEM((s + 2 * CONV_PAD, CONV_C), F32)],
        compiler_params=_params(("arbitrary",), 48),
        name="conformer_conv",
    )(z, lw['w_dw'], lw['b_dw'], lw['g_cln'], lw['b_cln'])


def _merge_kernel(a_ref, f_ref, c_ref, zg_ref, x_ref, mod_ref, g2_ref, wa_ref, wf_ref, wc_ref, wo_ref,
                  xn_ref, h2_ref):
    dot = functools.partial(jnp.dot, preferred_element_type=F32)
    m = jax.nn.sigmoid(zg_ref[0, :, 0:D]) * dot(a_ref[0].astype(BF16), wa_ref[...])
    m = m + jax.nn.sigmoid(zg_ref[0, :, D:2 * D]) * dot(f_ref[0].astype(BF16), wf_ref[...])
    m = m + jax.nn.sigmoid(zg_ref[0, :, 2 * D:3 * D]) * dot(c_ref[0].astype(BF16), wc_ref[...])
    y = dot(m.astype(BF16), wo_ref[...])
    xn = x_ref[0] + mod_ref[0, :, 2 * D:3 * D] * y
    xn_ref[0] = xn
    h2_ref[0] = _rms(xn, D) * g2_ref[...] * (1.0 + mod_ref[0, :, 4 * D:5 * D]) + mod_ref[0, :, 3 * D:4 * D]


def _merge(a_o, fm, cv, zg, x, mods, mod_row, g2, lw):
    b, s, _ = x.shape
    tm = 256
    tok = lambda w: pl.BlockSpec((1, tm, w), lambda bi, i: (bi, i, 0))
    full = lambda r, c: pl.BlockSpec((r, c), lambda bi, i: (0, 0))
    return pl.pallas_call(
        _merge_kernel,
        grid=(b, s // tm),
        in_specs=[tok(HEADS * HP), tok(FNET_C), tok(CONV_C), tok(3 * D), tok(D),
                  pl.BlockSpec((1, 1, 6 * D), lambda bi, i: (mod_row(bi), 0, 0)), full(1, D),
                  full(HEADS * HP, D), full(FNET_C, D), full(CONV_C, D), full(D, D)],
        out_specs=[tok(D), tok(D)],
        out_shape=[jax.ShapeDtypeStruct((b, s, D), F32)] * 2,
        compiler_params=_params(("arbitrary", "arbitrary"), 48),
        name="merge",
    )(a_o, fm, cv, zg, x, mods, g2.reshape(1, D), lw['wb_attn'], lw['wb_fnet'], lw['wb_conv'], lw['w_out'])


def _top_keyed(s, key, k):
    vals, keys = [], []
    for _ in range(k):
        m = jnp.max(s, axis=0, keepdims=True)
        i = jnp.min(jnp.where(s == m, key, jnp.int32(2 ** 30)), axis=0, keepdims=True)
        vals.append(m)
        keys.append(i)
        s = jnp.where(key == i, -jnp.inf, s)
    return jnp.concatenate(vals, axis=0), jnp.concatenate(keys, axis=0)


def _top_rows(s, k):
    return _top_keyed(s, lax.broadcasted_iota(I32, s.shape, 0), k)


CAND_A = 4
CAND_B = PEER_TOPK // (CAND_A + 1)


def _candidates(s1, s2, i1, i2):
    rank = lax.broadcasted_iota(I32, s1.shape, 0)
    cs, ck, ci = [], [], []
    for a in range(CAND_A):
        ok = rank < PEER_TOPK // (a + 1)
        cs.append(jnp.where(ok, s1[a:a + 1, :] + s2, -jnp.inf))
        ck.append(a * PEER_TOPK + rank)
        ci.append(i1[a:a + 1, :] * N_KEYS + i2)
    for b in range(CAND_B):
        ok = jnp.where(rank >= CAND_A, rank, PEER_TOPK) < PEER_TOPK // (b + 1)
        cs.append(jnp.where(ok, s1 + s2[b:b + 1, :], -jnp.inf))
        ck.append(rank * PEER_TOPK + b)
        ci.append(i1 * N_KEYS + i2[b:b + 1, :])
    return jnp.concatenate(cs, axis=0), jnp.concatenate(ck, axis=0), jnp.concatenate(ci, axis=0)


def _peer_score_kernel(h_ref, wq_ref, keys_ref, eid_ref, off_ref, gate_ref, qt_ref, ts_ref, ti_ref):
    tn = h_ref.shape[0]
    half = D_KEY // 2
    qt_ref[...] = lax.dot_general(wq_ref[...], h_ref[...], (((1,), (1,)), ((), ())), precision=HI,
                                  preferred_element_type=F32)

    def sub_scores(hp, carry):
        r0 = pl.multiple_of(hp * half, half)
        for lt in range(tn // LANES):
            ls = slice(lt * LANES, (lt + 1) * LANES)
            s = jnp.dot(keys_ref[hp], qt_ref[pl.ds(r0, half), ls], precision=HI, preferred_element_type=F32)
            v, i = _top_rows(s, PEER_TOPK)
            ts_ref[hp, :, ls] = v
            ti_ref[hp, :, ls] = i
        return carry

    lax.fori_loop(0, 2 * PEER_HEADS, sub_scores, 0)

    def head(h, carry):
        for lt in range(tn // LANES):
            ls = slice(lt * LANES, (lt + 1) * LANES)
            s1, s2 = ts_ref[2 * h, :, ls], ts_ref[2 * h + 1, :, ls]
            i1, i2 = ti_ref[2 * h, :, ls], ti_ref[2 * h + 1, :, ls]
            cand, flat, cid = _candidates(s1, s2, i1, i2)
            best, pos = _top_keyed(cand, flat, PEER_TOPK)
            eid = jnp.concatenate(
                [jnp.max(jnp.where(flat == pos[a:a + 1, :], cid, -1), axis=0, keepdims=True)
                 for a in range(PEER_TOPK)], axis=0)
            e = jnp.exp(best - best[0:1, :])
            eid_ref[h, :, ls] = eid
            off_ref[h, :, ls] = (eid >> 1) * SUBLANES
            gate_ref[h, :, ls] = e / jnp.sum(e, axis=0, keepdims=True)
        return carry

    lax.fori_loop(0, PEER_HEADS, head, 0)


def _peer_scores(h2, wq_t, keys):
    t = h2.shape[0]
    tn = 256
    out_blk = pl.BlockSpec((PEER_HEADS, PEER_TOPK, tn), lambda i: (0, 0, i))
    return pl.pallas_call(
        _peer_score_kernel,
        grid=(t // tn,),
        in_specs=[pl.BlockSpec((tn, D), lambda i: (i, 0)),
                  pl.BlockSpec((PEER_HEADS * D_KEY, D), lambda i: (0, 0)),
                  pl.BlockSpec((2 * PEER_HEADS, N_KEYS, D_KEY // 2), lambda i: (0, 0, 0))],
        out_specs=[out_blk, out_blk, out_blk],
        out_shape=[jax.ShapeDtypeStruct((PEER_HEADS, PEER_TOPK, t), I32),
                   jax.ShapeDtypeStruct((PEER_HEADS, PEER_TOPK, t), I32),
                   jax.ShapeDtypeStruct((PEER_HEADS, PEER_TOPK, t), F32)],
        scratch_shapes=[pltpu.VMEM((PEER_HEADS * D_KEY, tn), F32),
                        pltpu.VMEM((2 * PEER_HEADS, PEER_TOPK, tn), F32),
                        pltpu.VMEM((2 * PEER_HEADS, PEER_TOPK, tn), I32)],
        compiler_params=_params(("arbitrary",), 32),
        name="peer_scores",
    )(h2, wq_t, keys)


PEER_TN = 128
STAGE_ROWS = PEER_SEL * 2 * SUBLANES
NT_DIMS = (((1,), (1,)), ((), ()))


def _stage_layout():
    s = lax.broadcasted_iota(I32, (SUBLANES, LANES), 0)
    lo = lax.bitcast_convert_type((s + 1).astype(F32), U32) >> 16
    hi = lax.bitcast_convert_type((s + 1 + SUBLANES).astype(F32), U32) & jnp.uint32(0xFFFF0000)
    probe = pltpu.bitcast(hi | lo, BF16)
    code16 = lax.dot_general(jnp.ones((SUBLANES, LANES), BF16), probe, NT_DIMS,
                             preferred_element_type=F32) * (1.0 / LANES)
    r = lax.broadcasted_iota(I32, (2 * SUBLANES, STAGE_ROWS), 0)
    q = lax.broadcasted_iota(I32, (2 * SUBLANES, STAGE_ROWS), 1)
    tile16 = ((q & (2 * SUBLANES - 1)) == r).astype(F32)
    code = jnp.dot(code16, tile16, precision=HI, preferred_element_type=F32).astype(I32) - 1
    src_sub = code & (SUBLANES - 1)
    chunk = (src_sub & 3) + 4 * (code >> 3)
    return chunk, src_sub >> 2


def _layout_tables(kmask_ref, e8_ref):
    chunk, pair = _stage_layout()
    sub = lax.broadcasted_iota(I32, (SUBLANES, STAGE_ROWS), 0)
    kmask_ref[...] = (sub == chunk).astype(F32)
    a = lax.broadcasted_iota(I32, (2 * PEER_SEL, STAGE_ROWS), 0)
    q = lax.broadcasted_iota(I32, (2 * PEER_SEL, STAGE_ROWS), 1)
    e8_ref[...] = (((q >> 4) == (a >> 1)) & (pair[0:1, :] == (a & 1))).astype(BF16)


def _split3(x):
    x1 = x.astype(BF16)
    r1 = x - x1.astype(F32)
    x2 = r1.astype(BF16)
    return x1, x2, (r1 - x2.astype(F32)).astype(BF16)


def _stage_token(t, off_smem, tab_vmem, g_ref):
    for j in range(PEER_SEL):
        off = pl.multiple_of(off_smem[t, j], SUBLANES)
        g_ref[j * SUBLANES:(j + 1) * SUBLANES, :] = tab_vmem[pl.ds(off, SUBLANES), :]


def _staged(g_ref):
    return pltpu.bitcast(g_ref[...], BF16)


PIPE = 4


def _token_pipeline(stage, compute):
    for b in range(PIPE):
        stage(b, b)

    def group(k, carry):
        t = 2 * PIPE * k
        for b in range(PIPE):
            stage(t + PIPE + b, PIPE + b)
        for b in range(PIPE):
            compute(t + b, b)
        for b in range(PIPE):
            stage(jnp.minimum(t + 2 * PIPE + b, PEER_TN - 1), b)
        for b in range(PIPE):
            compute(t + PIPE + b, PIPE + b)
        return carry

    lax.fori_loop(0, PEER_TN // (2 * PIPE), group, 0)


def _sublane_allsum(p):
    p = p + pltpu.roll(p, 4, 0)
    p = p + pltpu.roll(p, 2, 0)
    return p + pltpu.roll(p, 1, 0)


def _load_tile_indices(off_hbm, off_smem, sem):
    i = pl.program_id(0)
    cp = pltpu.make_async_copy(off_hbm.at[pl.ds(i * PEER_TN, PEER_TN)], off_smem, sem)
    cp.start()
    cp.wait()


def _load_table_once(tab_hbm, tab_vmem, sem):
    @pl.when(pl.program_id(0) == 0)
    def _():
        cp = pltpu.make_async_copy(tab_hbm, tab_vmem, sem)
        cp.start()
        cp.wait()


def _gather_scratch(table_shape):
    return [pltpu.VMEM(table_shape, U32), pltpu.SMEM((PEER_TN, PEER_SEL), I32),
            *[pltpu.VMEM((PEER_SEL * SUBLANES, LANES), U32) for _ in range(2 * PIPE)],
            pltpu.VMEM((SUBLANES, STAGE_ROWS), F32), pltpu.VMEM((2 * PEER_SEL, STAGE_ROWS), BF16),
            pltpu.SemaphoreType.DMA((2,))]


def _gather_prologue(off_hbm, tab_hbm, tab_vmem, off_smem, kmask_ref, e8_ref, sems):
    _load_table_once(tab_hbm, tab_vmem, sems.at[0])
    _load_tile_indices(off_hbm, off_smem, sems.at[1])

    @pl.when(pl.program_id(0) == 0)
    def _():
        _layout_tables(kmask_ref, e8_ref)


def _peer_u_kernel(off_hbm, tab_hbm, h3_ref, eid_ref, gate_ref, w2_ref,
                   tab_vmem, off_smem, *rest):
    bufs, (kmask_ref, e8_ref, sems, s_ref) = rest[:2 * PIPE], rest[2 * PIPE:]
    _gather_prologue(off_hbm, tab_hbm, tab_vmem, off_smem, kmask_ref, e8_ref, sems)
    sub = lax.broadcasted_iota(I32, (SUBLANES, STAGE_ROWS), 0)
    s_ref[...] = jnp.zeros(s_ref.shape, F32)

    def compute(t, b):
        h = h3_ref[t]
        h_hi = h.astype(BF16)
        h_lo = (h - h_hi.astype(F32)).astype(BF16)
        r = lax.dot_general(jnp.concatenate([h_hi, h_lo], axis=0), _staged(bufs[b]), NT_DIMS,
                            preferred_element_type=F32)
        tot = _sublane_allsum((r[0:SUBLANES] + r[SUBLANES:]) * kmask_ref[...])
        t8 = pl.multiple_of((t >> 3) << 3, SUBLANES)
        s_ref[pl.ds(t8, SUBLANES), :] = jnp.where(sub == (t & 7), tot, s_ref[pl.ds(t8, SUBLANES), :])

    _token_pipeline(lambda t, b: _stage_token(t, off_smem, tab_vmem, bufs[b]), compute)
    e8 = e8_ref[...]
    act2 = sum(lax.dot_general(sk, e8, NT_DIMS, preferred_element_type=F32) for sk in _split3(s_ref[...]))
    j = lax.broadcasted_iota(I32, (PEER_SEL, 2 * PEER_SEL), 0)
    a = lax.broadcasted_iota(I32, (PEER_SEL, 2 * PEER_SEL), 1)
    dup = ((a >> 1) == j).astype(BF16)
    gate2 = sum(jnp.dot(gk, dup, preferred_element_type=F32) for gk in _split3(gate_ref[...]))
    pair2 = jnp.dot((eid_ref[...] & 1).astype(BF16), dup, preferred_element_type=F32)
    slot_pair = (lax.broadcasted_iota(I32, act2.shape, 1) & 1).astype(F32)
    w2_ref[...] = jnp.where(pair2 == slot_pair, jax.nn.gelu(act2) * gate2, 0.0)


def _peer_u(off_tok, eid_tok, gate_tok, u_packed, h3):
    t = off_tok.shape[0]
    tok = lambda w: pl.BlockSpec((PEER_TN, w), lambda i: (i, 0))
    any_spec = pl.BlockSpec(memory_space=pl.ANY)
    return pl.pallas_call(
        _peer_u_kernel,
        grid=(t // PEER_TN,),
        in_specs=[any_spec, any_spec, pl.BlockSpec((PEER_TN, SUBLANES, LANES), lambda i: (i, 0, 0)),
                  tok(PEER_SEL), tok(PEER_SEL)],
        out_specs=tok(2 * PEER_SEL),
        out_shape=jax.ShapeDtypeStruct((t, 2 * PEER_SEL), F32),
        scratch_shapes=_gather_scratch(u_packed.shape) + [pltpu.VMEM((PEER_TN, STAGE_ROWS), F32)],
        compiler_params=_params(("arbitrary",), 52),
        name="peer_u",
    )(off_tok, u_packed, h3, eid_tok, gate_tok)


def _peer_v_kernel(off_hbm, tab_hbm, w2_ref, x3_ref, gt_ref, o_ref,
                   tab_vmem, off_smem, *rest):
    bufs, (kmask_ref, e8_ref, sems, wx_ref) = rest[:2 * PIPE], rest[2 * PIPE:]
    _gather_prologue(off_hbm, tab_hbm, tab_vmem, off_smem, kmask_ref, e8_ref, sems)
    w2 = w2_ref[...]
    w_hi = w2.astype(BF16)
    w_lo = (w2 - w_hi.astype(F32)).astype(BF16)
    e8 = e8_ref[...]
    wx_ref[0] = jnp.dot(w_hi, e8, preferred_element_type=F32)
    wx_ref[1] = jnp.dot(w_lo, e8, preferred_element_type=F32)
    sub = lax.broadcasted_iota(I32, (SUBLANES, STAGE_ROWS), 0)

    def compute(t, b):
        t8 = pl.multiple_of((t >> 3) << 3, SUBLANES)
        pick = sub == (t & 7)
        kmask = kmask_ref[...]
        rows = []
        for k in range(2):
            w = _sublane_allsum(jnp.where(pick, wx_ref[k, pl.ds(t8, SUBLANES), :], 0.0))
            rows.append((w * kmask).astype(BF16))
        o = jnp.dot(jnp.concatenate(rows, axis=0), _staged(bufs[b]), preferred_element_type=F32)
        o_ref[t] = x3_ref[t] + gt_ref[0] * (o[0:SUBLANES] + o[SUBLANES:])

    _token_pipeline(lambda t, b: _stage_token(t, off_smem, tab_vmem, bufs[b]), compute)


def _peer_v(off_tok, w2, v_packed, x3, gt3, gate_row):
    t = off_tok.shape[0]
    any_spec = pl.BlockSpec(memory_space=pl.ANY)
    tok = pl.BlockSpec((PEER_TN, SUBLANES, LANES), lambda i: (i, 0, 0))
    return pl.pallas_call(
        _peer_v_kernel,
        grid=(t // PEER_TN,),
        in_specs=[any_spec, any_spec, pl.BlockSpec((PEER_TN, 2 * PEER_SEL), lambda i: (i, 0)), tok,
                  pl.BlockSpec((1, SUBLANES, LANES), lambda i: (gate_row(i), 0, 0))],
        out_specs=tok,
        out_shape=jax.ShapeDtypeStruct(x3.shape, F32),
        scratch_shapes=_gather_scratch(v_packed.shape) + [pltpu.VMEM((2, PEER_TN, STAGE_ROWS), F32)],
        compiler_params=_params(("arbitrary",), 52),
        name="peer_v",
    )(off_tok, v_packed, w2, x3, gt3)


def _pack_table(tab):
    n = tab.shape[0]
    bits = lax.bitcast_convert_type(tab.astype(BF16), jnp.uint16).astype(U32)
    words = bits[:, :D // 2] | (bits[:, D // 2:] << 16)
    return words.reshape(n * D // (2 * LANES), LANES)


def _peer(xn, h2, gt3, gate_row, pw):
    b, s, _ = xn.shape
    t = b * s
    eid_sel, off_sel, gate_sel = _peer_scores(h2.reshape(t, D), pw['wq_t'], pw['keys'])
    eid_tok, off_tok, gate_tok = (a.reshape(PEER_SEL, t).T for a in (eid_sel, off_sel, gate_sel))
    w2 = _peer_u(off_tok, eid_tok, gate_tok, pw['u_packed'], h2.reshape(t, SUBLANES, LANES))
    out = _peer_v(off_tok, w2, pw['v_packed'], xn.reshape(t, SUBLANES, LANES), gt3,
                  functools.partial(gate_row, tiles_per_seq=s // PEER_TN))
    return out.reshape(b, s, D)


def _rope_tables(s):
    pos = np.arange(s)
    half = QK_ROPE // 2
    inv = ROPE_BASE ** (-np.arange(0, half, 2, dtype=np.float64) / half)
    ar = (pos // GRID_W)[:, None] * inv
    ac = (pos % GRID_W)[:, None] * inv
    ang = np.concatenate([ar, ar, ac, ac], axis=-1)
    cos = np.ones((s, HP), np.float32)
    sin_a = np.zeros((s, HP), np.float32)
    sin_b = np.zeros((s, HP), np.float32)
    cos[:, QK_NOPE:QK_HEAD] = np.cos(ang)
    sin = np.sin(ang)
    first = (np.arange(QK_ROPE) % (half)) < half // 2
    sin_a[:, QK_NOPE:QK_HEAD] = np.where(first, -sin, 0.0)
    sin_b[:, QK_NOPE:QK_HEAD] = np.where(first, 0.0, sin)
    return jnp.asarray(cos), jnp.asarray(sin_a), jnp.asarray(sin_b)


def _identity_tables(s):
    return jnp.ones((s, HP), F32), jnp.zeros((s, HP), F32), jnp.zeros((s, HP), F32)


def _pad_heads(w, width):
    k = w.shape[0]
    w = w.reshape(k, HEADS, width)
    return jnp.pad(w, ((0, 0), (0, 0), (0, HP - width))).reshape(k, HEADS * HP)


def _pad_lanes(g):
    return jnp.pad(g, (0, HP - g.shape[0])).reshape(1, HP)


def _layer_weights(i, w_in, g_ckv, w_ukv, g_cq, w_uq, g_qn, g_kn, w_dw, b_dw, g_cln, b_cln,
                   wb_attn, wb_fnet, wb_conv, w_out):
    win = w_in[i]
    w_pad = jnp.concatenate([win[:, :KV_IN], jnp.zeros((D, KV_IN_PAD - KV_IN), F32), win[:, KV_IN:]], axis=1)
    ukv = w_ukv[i].reshape(KV_LORA, HEADS, QK_NOPE + V_HEAD)
    wb_a = wb_attn[i].reshape(HEADS, V_HEAD, D)
    return {
        'w_in': w_pad.astype(BF16),
        'g_ckv': g_ckv[i].reshape(1, KV_LORA), 'g_cq': g_cq[i].reshape(1, Q_LORA),
        'g_qn': _pad_lanes(g_qn[i]), 'g_kn': _pad_lanes(g_kn[i]),
        'w_uk': _pad_heads(ukv[:, :, :QK_NOPE].reshape(KV_LORA, HEADS * QK_NOPE), QK_NOPE).astype(BF16),
        'w_uv': _pad_heads(ukv[:, :, QK_NOPE:].reshape(KV_LORA, HEADS * V_HEAD), V_HEAD).astype(BF16),
        'w_uq': _pad_heads(w_uq[i], QK_HEAD).astype(BF16),
        'w_dw': jnp.pad(w_dw[i].reshape(CONV_W, CONV_C), ((0, 1), (0, 0))),
        'b_dw': b_dw[i].reshape(1, CONV_C), 'g_cln': g_cln[i].reshape(1, CONV_C),
        'b_cln': b_cln[i].reshape(1, CONV_C),
        'wb_attn': jnp.pad(wb_a, ((0, 0), (0, HP - V_HEAD), (0, 0))).reshape(HEADS * HP, D).astype(BF16),
        'wb_fnet': wb_fnet[i].astype(BF16), 'wb_conv': wb_conv[i].astype(BF16), 'w_out': w_out[i].astype(BF16),
    }


def _mixer_branches(zf, zc, lw):
    fm = _fourier_latent(zf) if zf.shape[1] == FFT_N * FFT_N else _fourier_direct(zf)
    return fm, _conformer_conv(zc, lw)


def kernel(x, c, ctx, c_ctx, w_mod, b_mod, g_norm1, w_in, g_ckv, w_ukv, g_cq, w_uq, g_qn, g_kn, w_dw, b_dw,
           g_cln, b_cln, wb_attn, wb_fnet, wb_conv, w_out, g_norm2, w_query, sub_keys, u_tab, v_tab):
    b, s, _ = x.shape
    n_ctx = ctx.shape[1]
    depth = w_mod.shape[0]
    ctx_row = b
    cond = jnp.concatenate([c, c_ctx[None, :], jnp.zeros((16 - b - 1, D), F32)], axis=0)
    mods_all = _adaln(cond, w_mod, b_mod)
    rope = _rope_tables(s)
    no_rope = _identity_tables(n_ctx)
    x_row = lambda bi: bi
    c_row = lambda bi: ctx_row
    x_gate_row = lambda i, tiles_per_seq: i // tiles_per_seq
    c_gate_row = lambda i, tiles_per_seq: ctx_row
    pos_tiles = s // 256
    for i in range(depth):
        lw = _layer_weights(i, w_in, g_ckv, w_ukv, g_cq, w_uq, g_qn, g_kn, w_dw, b_dw, g_cln, b_cln,
                            wb_attn, wb_fnet, wb_conv, w_out)
        pw = {'wq_t': w_query[i].T, 'keys': sub_keys[i].reshape(2 * PEER_HEADS, N_KEYS, D_KEY // 2),
              'u_packed': _pack_table(u_tab[i]), 'v_packed': _pack_table(v_tab[i])}
        mods = mods_all[i].reshape(16, 1, 6 * D)
        gt3 = mods_all[i][:, 5 * D:6 * D].reshape(16, SUBLANES, LANES)
        update_ctx = i < depth - 1

        kv_x, cq_x, zf_x, zc_x, zg_x = _inproj(x, mods, x_row, g_norm1[i], lw['w_in'], IN_SEGS)
        if update_ctx:
            kv_c, cq_c, zf_c, zc_c, zg_c = _inproj(ctx, mods, c_row, g_norm1[i], lw['w_in'], IN_SEGS)
        else:
            kv_c, cq_c = _inproj(ctx, mods, c_row, g_norm1[i], lw['w_in'][:, :KV_IN_PAD + Q_LORA], IN_SEGS[:2])
        q_c, k_c, v_c = _mla_prep(kv_c, cq_c, no_rope, lambda t: 0, lw)
        q_x, k_x, v_x = _mla_prep(kv_x, cq_x, rope, lambda t: t % pos_tiles, lw)

        a_x = _attention(q_x, [(k_c, v_c), (k_x, v_x)])
        fm_x, cv_x = _mixer_branches(zf_x, zc_x, lw)
        xn, h2 = _merge(a_x, fm_x, cv_x, zg_x, x, mods, x_row, g_norm2[i], lw)
        x = _peer(xn, h2, gt3, x_gate_row, pw)
        if update_ctx:
            a_c = _attention(q_c, [(k_c, v_c)])
            fm_c, cv_c = _mixer_branches(zf_c, zc_c, lw)
            cn, ch2 = _merge(a_c, fm_c, cv_c, zg_c, ctx, mods, c_row, g_norm2[i], lw)
            ctx = _peer(cn, ch2, gt3, c_gate_row, pw)
    return x
```

```python
import functools

import numpy as np
import jax
import jax.numpy as jnp
from jax import lax
from jax.experimental import pallas as pl
from jax.experimental.pallas import tpu as pltpu

F32 = jnp.float32
BF16 = jnp.bfloat16
I32 = jnp.int32
U32 = jnp.uint32
HI = lax.Precision.HIGHEST

LANES = 128
SUBLANES = 8
VMEM_BYTES_V7X = 64 * 1024 * 1024

D = 1024
GRID_W = 64
EPS = 1e-6
HEADS = 8
Q_LORA = 256
KV_LORA = 128
QK_NOPE = 64
QK_ROPE = 32
V_HEAD = 64
QK_HEAD = QK_NOPE + QK_ROPE
ROPE_BASE = 10000.0
HP = LANES
FNET_GROUPS = 4
FNET_GROUP_DIM = 64
FNET_C = FNET_GROUPS * FNET_GROUP_DIM
CONV_C = 256
CONV_W = 31
CONV_PAD = 16
PEER_HEADS = 8
N_KEYS = 128
D_KEY = 128
PEER_TOPK = 16
PEER_SEL = PEER_HEADS * PEER_TOPK
KV_IN = KV_LORA + QK_ROPE
KV_IN_PAD = 2 * LANES
IN_SEGS = (KV_IN_PAD, Q_LORA, FNET_C, 2 * CONV_C, 3 * D)
FFT_N = 64
NEG = -1e30


def _params(sem, vmem_mb):
    return pltpu.CompilerParams(dimension_semantics=sem, vmem_limit_bytes=vmem_mb * 1024 * 1024)


def _rms(x, n):
    return x * lax.rsqrt(jnp.sum(x * x, axis=-1, keepdims=True) * (1.0 / n) + EPS)


def _adaln_kernel(c_ref, w_ref, b_ref, o_ref):
    c = c_ref[...]
    s = c * jax.nn.sigmoid(c)
    o_ref[0] = jnp.dot(s, w_ref[0], precision=HI, preferred_element_type=F32) + b_ref[0]


def _adaln(cond, w_mod, b_mod):
    depth = w_mod.shape[0]
    tn = 512
    return pl.pallas_call(
        _adaln_kernel,
        grid=(depth, 6 * D // tn),
        in_specs=[
            pl.BlockSpec((16, D), lambda l, j: (0, 0)),
            pl.BlockSpec((1, D, tn), lambda l, j: (l, 0, j)),
            pl.BlockSpec((1, 1, tn), lambda l, j: (l, 0, j)),
        ],
        out_specs=pl.BlockSpec((1, 16, tn), lambda l, j: (l, 0, j)),
        out_shape=jax.ShapeDtypeStruct((depth, 16, 6 * D), F32),
        compiler_params=_params(("arbitrary", "arbitrary"), 32),
        name="adaln",
    )(cond, w_mod, b_mod.reshape(depth, 1, 6 * D))


def _inproj_kernel(x_ref, mod_ref, g_ref, w_ref, *out_refs, widths):
    x = x_ref[0]
    y = _rms(x, D) * g_ref[...]
    h = (y * (1.0 + mod_ref[0, :, D:2 * D]) + mod_ref[0, :, 0:D]).astype(BF16)
    col = 0
    for o_ref, width in zip(out_refs, widths):
        for c0 in range(0, width, D):
            c1 = min(c0 + D, width)
            o_ref[0, :, c0:c1] = jnp.dot(h, w_ref[:, col + c0:col + c1], preferred_element_type=F32)
        col += width


def _inproj(x, mods, mod_row, g, w_pad, widths):
    b, s, _ = x.shape
    tm = 256
    ncol = sum(widths)
    return pl.pallas_call(
        functools.partial(_inproj_kernel, widths=widths),
        grid=(b, s // tm),
        in_specs=[
            pl.BlockSpec((1, tm, D), lambda bi, i: (bi, i, 0)),
            pl.BlockSpec((1, 1, 6 * D), lambda bi, i: (mod_row(bi), 0, 0)),
            pl.BlockSpec((1, D), lambda bi, i: (0, 0)),
            pl.BlockSpec((D, ncol), lambda bi, i: (0, 0)),
        ],
        out_specs=[pl.BlockSpec((1, tm, w), lambda bi, i: (bi, i, 0)) for w in widths],
        out_shape=[jax.ShapeDtypeStruct((b, s, w), F32) for w in widths],
        compiler_params=_params(("arbitrary", "arbitrary"), 48),
        name="inproj",
    )(x, mods, g.reshape(1, D), w_pad)


def _rope(x, cos, sin_a, sin_b):
    return x * cos + pltpu.roll(x, LANES - 8, 1) * sin_a + pltpu.roll(x, 8, 1) * sin_b


def _mla_prep_kernel(kv_ref, cq_ref, cos_ref, sa_ref, sb_ref, gckv_ref, gcq_ref, gqn_ref, gkn_ref,
                     wuk_ref, wuv_ref, wuq_ref, q_ref, k_ref, v_ref):
    kvx = kv_ref[0]
    ckv = (_rms(kvx[:, :KV_LORA], KV_LORA) * gckv_ref[...]).astype(BF16)
    k_nope = jnp.dot(ckv, wuk_ref[...], preferred_element_type=F32)
    lane = lax.broadcasted_iota(I32, (1, HEADS * HP), 1)
    ones_lane = ((lane & (HP - 1)) == V_HEAD).astype(F32)
    v_ref[0] = (jnp.dot(ckv, wuv_ref[...], preferred_element_type=F32) + ones_lane).astype(BF16)
    k_rope = pltpu.roll(kvx[:, KV_LORA:], QK_NOPE, 1)
    cq = (_rms(cq_ref[0], Q_LORA) * gcq_ref[...]).astype(BF16)
    q = jnp.dot(cq, wuq_ref[...], preferred_element_type=F32)
    cos, sin_a, sin_b = cos_ref[...], sa_ref[...], sb_ref[...]
    for h in range(HEADS):
        sl = slice(h * HP, (h + 1) * HP)
        kh = k_nope[:, sl] + k_rope
        kh = _rms(kh, QK_HEAD) * gkn_ref[...]
        k_ref[0, :, sl] = _rope(kh, cos, sin_a, sin_b).astype(BF16)
        qh = _rms(q[:, sl], QK_HEAD) * gqn_ref[...]
        q_ref[0, :, sl] = (_rope(qh, cos, sin_a, sin_b) * QK_HEAD ** -0.5).astype(BF16)


def _mla_prep(kv, cq, tables, pos_block, lw):
    b, s, _ = kv.shape
    tm = 256
    full = lambda shape: pl.BlockSpec(shape, lambda bi, i: (0,) * len(shape))
    tok = lambda w: pl.BlockSpec((1, tm, w), lambda bi, i: (bi, i, 0))
    tab = pl.BlockSpec((tm, HP), lambda bi, i: (pos_block(i), 0))
    return pl.pallas_call(
        _mla_prep_kernel,
        grid=(b, s // tm),
        in_specs=[tok(KV_IN_PAD), tok(Q_LORA), tab, tab, tab,
                  full((1, KV_LORA)), full((1, Q_LORA)), full((1, HP)), full((1, HP)),
                  full((KV_LORA, HEADS * HP)), full((KV_LORA, HEADS * HP)), full((Q_LORA, HEADS * HP))],
        out_specs=[tok(HEADS * HP)] * 3,
        out_shape=[jax.ShapeDtypeStruct((b, s, HEADS * HP), BF16)] * 3,
        compiler_params=_params(("arbitrary", "arbitrary"), 40),
        name="mla_prep",
    )(kv, cq, *tables, lw['g_ckv'], lw['g_cq'], lw['g_qn'], lw['g_kn'], lw['w_uk'], lw['w_uv'], lw['w_uq'])


def _attn_kernel(q_ref, *refs, n_seg, chunk, hgroup):
    o_ref = refs[2 * n_seg]
    tq = q_ref.shape[1]
    for h0 in range(0, HEADS, hgroup):
        sls = [slice(h * HP, (h + 1) * HP) for h in range(h0, h0 + hgroup)]
        qs = [q_ref[0, :, sl] for sl in sls]
        carry = tuple((jnp.full((tq, 1), NEG, F32), jnp.zeros((tq, HP), F32)) for _ in sls)
        for s in range(n_seg):
            k_ref, v_ref = refs[2 * s], refs[2 * s + 1]
            ck = min(chunk, k_ref.shape[1])

            def body(c, carry, k_ref=k_ref, v_ref=v_ref, ck=ck):
                k0 = pl.multiple_of(c * ck, ck)
                out = []
                for (m, acc), qh, sl in zip(carry, qs, sls):
                    kc = k_ref[0, pl.ds(k0, ck), sl]
                    vc = v_ref[0, pl.ds(k0, ck), sl]
                    sc = lax.dot_general(qh, kc, (((1,), (1,)), ((), ())), preferred_element_type=F32)
                    m_new = jnp.maximum(m, jnp.max(sc, axis=-1, keepdims=True))
                    p = jnp.exp(sc - m_new).astype(BF16)
                    acc = jnp.exp(m - m_new) * acc + jnp.dot(p, vc, preferred_element_type=F32)
                    out.append((m_new, acc))
                return tuple(out)

            carry = lax.fori_loop(0, k_ref.shape[1] // ck, body, carry)
        for (_, acc), sl in zip(carry, sls):
            o_ref[0, :, sl] = acc / acc[:, V_HEAD:V_HEAD + 1]


def _attention(q, kv_segs):
    b, sq, w = q.shape
    tq = 256
    in_specs = [pl.BlockSpec((1, tq, w), lambda bi, i: (bi, i, 0))]
    args = [q]
    for k, v in kv_segs:
        sk = k.shape[1]
        in_specs += [pl.BlockSpec((1, sk, w), lambda bi, i: (bi, 0, 0))] * 2
        args += [k, v]
    return pl.pallas_call(
        functools.partial(_attn_kernel, n_seg=len(kv_segs), chunk=512, hgroup=4),
        grid=(b, sq // tq),
        in_specs=in_specs,
        out_specs=pl.BlockSpec((1, tq, w), lambda bi, i: (bi, i, 0)),
        out_shape=jax.ShapeDtypeStruct((b, sq, w), F32),
        compiler_params=_params(("arbitrary", "arbitrary"), 56),
        name="attention",
    )(*args)


def _dft_mats(n):
    k = np.arange(n)
    ang = 2.0 * np.pi * ((k[:, None] * k[None, :]) % n) / n
    return np.cos(ang).astype(np.float32), np.sin(ang).astype(np.float32)


def _channel_dft_mats():
    c, s = _dft_mats(FNET_GROUP_DIM)
    eye = np.eye(FNET_GROUPS, dtype=np.float32)
    return np.kron(eye, c), np.kron(eye, s)


def _fft1_kernel(x_ref, c_ref, s_ref, tr_ref, ti_ref, br_ref, bi_ref):
    x = x_ref[0]
    ar = jnp.dot(c_ref[...], x, precision=HI, preferred_element_type=F32)
    ai = -jnp.dot(s_ref[...], x, precision=HI, preferred_element_type=F32)
    tr, ti = tr_ref[...], ti_ref[...]
    br_ref[0] = ar * tr - ai * ti
    bi_ref[0] = ar * ti + ai * tr


def _fft2_kernel(br_ref, bi_ref, c_ref, s_ref, cc_ref, sc_ref, o_ref, *, nk, scale):
    c, s = c_ref[...], s_ref[...]
    dot = functools.partial(jnp.dot, precision=HI, preferred_element_type=F32)
    for j in range(nk):
        br, bi = br_ref[0, j], bi_ref[0, j]
        yr = dot(c, br) + dot(s, bi)
        yi = dot(c, bi) - dot(s, br)
        o_ref[0, :, j * FNET_C:(j + 1) * FNET_C] = (dot(yr, cc_ref[...]) + dot(yi, sc_ref[...])) * scale


def _fourier_latent(z):
    b, s, _ = z.shape
    assert s == FFT_N * FFT_N
    n = FFT_N
    cn, sn = _dft_mats(n)
    k1 = np.arange(n)
    ang = 2.0 * np.pi * ((k1[:, None] * k1[None, :]) % s) / s
    tr = jnp.repeat(jnp.asarray(np.cos(ang), F32), FNET_C, axis=1)
    ti = jnp.repeat(jnp.asarray(-np.sin(ang), F32), FNET_C, axis=1)
    wide = n * FNET_C
    tl = 2048
    mat = pl.BlockSpec((n, n), lambda bi, j: (0, 0))
    br, bi = pl.pallas_call(
        _fft1_kernel,
        grid=(b, wide // tl),
        in_specs=[pl.BlockSpec((1, n, tl), lambda bi, j: (bi, 0, j)), mat, mat,
                  pl.BlockSpec((n, tl), lambda bi, j: (0, j)), pl.BlockSpec((n, tl), lambda bi, j: (0, j))],
        out_specs=[pl.BlockSpec((1, n, tl), lambda bi, j: (bi, 0, j))] * 2,
        out_shape=[jax.ShapeDtypeStruct((b, n, wide), F32)] * 2,
        compiler_params=_params(("arbitrary", "arbitrary"), 32),
        name="fft_stage1",
    )(z.reshape(b, n, wide), cn, sn, tr, ti)
    cc, sc = _channel_dft_mats()
    nk = 8
    cmat = pl.BlockSpec((FNET_C, FNET_C), lambda bi, j: (0, 0))
    blk = pl.BlockSpec((1, nk, n, FNET_C), lambda bi, j: (bi, j, 0, 0))
    out = pl.pallas_call(
        functools.partial(_fft2_kernel, nk=nk, scale=float((s * FNET_GROUP_DIM) ** -0.5)),
        grid=(b, n // nk),
        in_specs=[blk, blk, mat, mat, cmat, cmat],
        out_specs=pl.BlockSpec((1, n, nk * FNET_C), lambda bi, j: (bi, 0, j)),
        out_shape=jax.ShapeDtypeStruct((b, n, wide), F32),
        compiler_params=_params(("arbitrary", "arbitrary"), 32),
        name="fft_stage2",
    )(br.reshape(b, n, n, FNET_C), bi.reshape(b, n, n, FNET_C), cn, sn, cc, sc)
    return out.reshape(b, s, FNET_C)


def _dft_direct_kernel(x_ref, c_ref, s_ref, cc_ref, sc_ref, o_ref, *, scale):
    dot = functools.partial(jnp.dot, precision=HI, preferred_element_type=F32)
    x = x_ref[0]
    yr = dot(c_ref[...], x)
    yi = -dot(s_ref[...], x)
    o_ref[0] = (dot(yr, cc_ref[...]) + dot(yi, sc_ref[...])) * scale


def _fourier_direct(z):
    b, s, _ = z.shape
    cn, sn = _dft_mats(s)
    cc, sc = _channel_dft_mats()
    mat = pl.BlockSpec((s, s), lambda bi: (0, 0))
    cmat = pl.BlockSpec((FNET_C, FNET_C), lambda bi: (0, 0))
    blk = pl.BlockSpec((1, s, FNET_C), lambda bi: (bi, 0, 0))
    return pl.pallas_call(
        functools.partial(_dft_direct_kernel, scale=float((s * FNET_GROUP_DIM) ** -0.5)),
        grid=(b,),
        in_specs=[blk, mat, mat, cmat, cmat],
        out_specs=blk,
        out_shape=jax.ShapeDtypeStruct((b, s, FNET_C), F32),
        compiler_params=_params(("arbitrary",), 32),
        name="dft_direct",
    )(z, cn, sn, cc, sc)


def _conv_kernel(z_ref, w_ref, bdw_ref, g_ref, b_ref, o_ref, u_ref, *, rows):
    s = z_ref.shape[1]
    zero = jnp.zeros((CONV_PAD, CONV_C), F32)
    u_ref[0:CONV_PAD, :] = zero
    u_ref[CONV_PAD + s:CONV_PAD + s + CONV_PAD, :] = zero
    for r0 in range(0, s, rows):
        a = z_ref[0, r0:r0 + rows, 0:CONV_C]
        g = z_ref[0, r0:r0 + rows, CONV_C:2 * CONV_C]
        u_ref[CONV_PAD + r0:CONV_PAD + r0 + rows, :] = a * jax.nn.sigmoid(g)
    off = CONV_PAD - CONV_W // 2
    for r0 in range(0, s, rows):
        acc = jnp.zeros((rows, CONV_C), F32) + bdw_ref[...]
        for j in range(CONV_W):
            acc = acc + u_ref[off + r0 + j:off + r0 + j + rows, :] * w_ref[j:j + 1, :]
        mu = jnp.mean(acc, axis=-1, keepdims=True)
        d = acc - mu
        var = jnp.mean(d * d, axis=-1, keepdims=True)
        y = d * lax.rsqrt(var + EPS) * g_ref[...] + b_ref[...]
        o_ref[0, r0:r0 + rows, :] = y * jax.nn.sigmoid(y)


def _conformer_conv(z, lw):
    b, s, _ = z.shape
    vec = pl.BlockSpec((1, CONV_C), lambda bi: (0, 0))
    return pl.pallas_call(
        functools.partial(_conv_kernel, rows=min(256, s)),
        grid=(b,),
        in_specs=[pl.BlockSpec((1, s, 2 * CONV_C), lambda bi: (bi, 0, 0)),
                  pl.BlockSpec((CONV_W + 1, CONV_C), lambda bi: (0, 0)), vec, vec, vec],
        out_specs=pl.BlockSpec((1, s, CONV_C), lambda bi: (bi, 0, 0)),
        out_shape=jax.ShapeDtypeStruct((b, s, CONV_C), F32),
        scratch_shapes=[pltpu.VMEM((s + 2 * CONV_PAD, CONV_C), F32)],
        compiler_params=_params(("arbitrary",), 48),
        name="conformer_conv",
    )(z, lw['w_dw'], lw['b_dw'], lw['g_cln'], lw['b_cln'])


def _merge_kernel(a_ref, f_ref, c_ref, zg_ref, x_ref, mod_ref, g2_ref, wa_ref, wf_ref, wc_ref, wo_ref,
                  xn_ref, h2_ref):
    dot = functools.partial(jnp.dot, preferred_element_type=F32)
    m = jax.nn.sigmoid(zg_ref[0, :, 0:D]) * dot(a_ref[0].astype(BF16), wa_ref[...])
    m = m + jax.nn.sigmoid(zg_ref[0, :, D:2 * D]) * dot(f_ref[0].astype(BF16), wf_ref[...])
    m = m + jax.nn.sigmoid(zg_ref[0, :, 2 * D:3 * D]) * dot(c_ref[0].astype(BF16), wc_ref[...])
    y = dot(m.astype(BF16), wo_ref[...])
    xn = x_ref[0] + mod_ref[0, :, 2 * D:3 * D] * y
    xn_ref[0] = xn
    h2_ref[0] = _rms(xn, D) * g2_ref[...] * (1.0 + mod_ref[0, :, 4 * D:5 * D]) + mod_ref[0, :, 3 * D:4 * D]


def _merge(a_o, fm, cv, zg, x, mods, mod_row, g2, lw):
    b, s, _ = x.shape
    tm = 256
    tok = lambda w: pl.BlockSpec((1, tm, w), lambda bi, i: (bi, i, 0))
    full = lambda r, c: pl.BlockSpec((r, c), lambda bi, i: (0, 0))
    return pl.pallas_call(
        _merge_kernel,
        grid=(b, s // tm),
        in_specs=[tok(HEADS * HP), tok(FNET_C), tok(CONV_C), tok(3 * D), tok(D),
                  pl.BlockSpec((1, 1, 6 * D), lambda bi, i: (mod_row(bi), 0, 0)), full(1, D),
                  full(HEADS * HP, D), full(FNET_C, D), full(CONV_C, D), full(D, D)],
        out_specs=[tok(D), tok(D)],
        out_shape=[jax.ShapeDtypeStruct((b, s, D), F32)] * 2,
        compiler_params=_params(("arbitrary", "arbitrary"), 48),
        name="merge",
    )(a_o, fm, cv, zg, x, mods, g2.reshape(1, D), lw['wb_attn'], lw['wb_fnet'], lw['wb_conv'], lw['w_out'])


def _top_keyed(s, key, k):
    vals, keys = [], []
    for _ in range(k):
        m = jnp.max(s, axis=0, keepdims=True)
        i = jnp.min(jnp.where(s == m, key, jnp.int32(2 ** 30)), axis=0, keepdims=True)
        vals.append(m)
        keys.append(i)
        s = jnp.where(key == i, -jnp.inf, s)
    return jnp.concatenate(vals, axis=0), jnp.concatenate(keys, axis=0)


def _top_rows(s, k):
    return _top_keyed(s, lax.broadcasted_iota(I32, s.shape, 0), k)


CAND_A = 4
CAND_B = PEER_TOPK // (CAND_A + 1)


def _candidates(s1, s2, i1, i2):
    rank = lax.broadcasted_iota(I32, s1.shape, 0)
    cs, ck, ci = [], [], []
    for a in range(CAND_A):
        ok = rank < PEER_TOPK // (a + 1)
        cs.append(jnp.where(ok, s1[a:a + 1, :] + s2, -jnp.inf))
        ck.append(a * PEER_TOPK + rank)
        ci.append(i1[a:a + 1, :] * N_KEYS + i2)
    for b in range(CAND_B):
        ok = jnp.where(rank >= CAND_A, rank, PEER_TOPK) < PEER_TOPK // (b + 1)
        cs.append(jnp.where(ok, s1 + s2[b:b + 1, :], -jnp.inf))
        ck.append(rank * PEER_TOPK + b)
        ci.append(i1 * N_KEYS + i2[b:b + 1, :])
    return jnp.concatenate(cs, axis=0), jnp.concatenate(ck, axis=0), jnp.concatenate(ci, axis=0)


def _peer_score_kernel(h_ref, wqh_ref, wql_ref, keys_ref, eid_ref, off_ref, gate_ref, qt_ref):
    tn = h_ref.shape[0]
    half = D_KEY // 2
    h = h_ref[...]
    h_hi = h.astype(BF16)
    h_lo = (h - h_hi.astype(F32)).astype(BF16)
    nt = functools.partial(lax.dot_general, dimension_numbers=(((1,), (1,)), ((), ())), preferred_element_type=F32)
    qt_ref[...] = nt(wqh_ref[...], h_hi) + (nt(wqh_ref[...], h_lo) + nt(wql_ref[...], h_hi))

    def head(h, carry):
        for lt in range(tn // LANES):
            ls = slice(lt * LANES, (lt + 1) * LANES)
            tops = []
            for p in range(2):
                r0 = pl.multiple_of((2 * h + p) * half, half)
                s = jnp.dot(keys_ref[2 * h + p], qt_ref[pl.ds(r0, half), ls], precision=HI,
                            preferred_element_type=F32)
                tops.append(_top_rows(s, PEER_TOPK))
            (s1, i1), (s2, i2) = tops
            cand, flat, cid = _candidates(s1, s2, i1, i2)
            best, pos = _top_keyed(cand, flat, PEER_TOPK)
            eid = jnp.concatenate(
                [jnp.max(jnp.where(flat == pos[a:a + 1, :], cid, -1), axis=0, keepdims=True)
                 for a in range(PEER_TOPK)], axis=0)
            e = jnp.exp(best - best[0:1, :])
            eid_ref[h, :, ls] = eid
            off_ref[h, :, ls] = (eid >> 1) * SUBLANES
            gate_ref[h, :, ls] = e / jnp.sum(e, axis=0, keepdims=True)
        return carry

    lax.fori_loop(0, PEER_HEADS, head, 0)


def _peer_scores(h2, wq_t, keys):
    t = h2.shape[0]
    tn = 256
    out_blk = pl.BlockSpec((PEER_HEADS, PEER_TOPK, tn), lambda i: (0, 0, i))
    wq_hi = wq_t.astype(BF16)
    wq_lo = (wq_t - wq_hi.astype(F32)).astype(BF16)
    wq_blk = pl.BlockSpec((PEER_HEADS * D_KEY, D), lambda i: (0, 0))
    return pl.pallas_call(
        _peer_score_kernel,
        grid=(t // tn,),
        in_specs=[pl.BlockSpec((tn, D), lambda i: (i, 0)), wq_blk, wq_blk,
                  pl.BlockSpec((2 * PEER_HEADS, N_KEYS, D_KEY // 2), lambda i: (0, 0, 0))],
        out_specs=[out_blk, out_blk, out_blk],
        out_shape=[jax.ShapeDtypeStruct((PEER_HEADS, PEER_TOPK, t), I32),
                   jax.ShapeDtypeStruct((PEER_HEADS, PEER_TOPK, t), I32),
                   jax.ShapeDtypeStruct((PEER_HEADS, PEER_TOPK, t), F32)],
        scratch_shapes=[pltpu.VMEM((PEER_HEADS * D_KEY, tn), F32)],
        compiler_params=_params(("arbitrary",), 32),
        name="peer_scores",
    )(h2, wq_hi, wq_lo, keys)


PEER_TN = 128
STAGE_ROWS = PEER_SEL * 2 * SUBLANES
NT_DIMS = (((1,), (1,)), ((), ()))


def _stage_layout():
    s = lax.broadcasted_iota(I32, (SUBLANES, LANES), 0)
    lo = lax.bitcast_convert_type((s + 1).astype(F32), U32) >> 16
    hi = lax.bitcast_convert_type((s + 1 + SUBLANES).astype(F32), U32) & jnp.uint32(0xFFFF0000)
    probe = pltpu.bitcast(hi | lo, BF16)
    code16 = lax.dot_general(jnp.ones((SUBLANES, LANES), BF16), probe, NT_DIMS,
                             preferred_element_type=F32) * (1.0 / LANES)
    r = lax.broadcasted_iota(I32, (2 * SUBLANES, STAGE_ROWS), 0)
    q = lax.broadcasted_iota(I32, (2 * SUBLANES, STAGE_ROWS), 1)
    tile16 = ((q & (2 * SUBLANES - 1)) == r).astype(F32)
    code = jnp.dot(code16, tile16, precision=HI, preferred_element_type=F32).astype(I32) - 1
    src_sub = code & (SUBLANES - 1)
    chunk = (src_sub & 3) + 4 * (code >> 3)
    return chunk, src_sub >> 2


def _layout_tables(kmask_ref, e8_ref):
    chunk, pair = _stage_layout()
    sub = lax.broadcasted_iota(I32, (SUBLANES, STAGE_ROWS), 0)
    kmask_ref[...] = (sub == chunk).astype(F32)
    a = lax.broadcasted_iota(I32, (2 * PEER_SEL, STAGE_ROWS), 0)
    q = lax.broadcasted_iota(I32, (2 * PEER_SEL, STAGE_ROWS), 1)
    e8_ref[...] = (((q >> 4) == (a >> 1)) & (pair[0:1, :] == (a & 1))).astype(BF16)


def _split3(x):
    x1 = x.astype(BF16)
    r1 = x - x1.astype(F32)
    x2 = r1.astype(BF16)
    return x1, x2, (r1 - x2.astype(F32)).astype(BF16)


def _stage_token(t, off_smem, tab_vmem, g_ref):
    for j in range(PEER_SEL):
        off = pl.multiple_of(off_smem[t, j], SUBLANES)
        g_ref[j * SUBLANES:(j + 1) * SUBLANES, :] = tab_vmem[pl.ds(off, SUBLANES), :]


def _staged(g_ref):
    return pltpu.bitcast(g_ref[...], BF16)


PIPE = 4


def _token_pipeline(stage, compute):
    for b in range(PIPE):
        stage(b, b)

    def group(k, carry):
        t = 2 * PIPE * k
        for b in range(PIPE):
            stage(t + PIPE + b, PIPE + b)
        for b in range(PIPE):
            compute(t + b, b)
        for b in range(PIPE):
            stage(jnp.minimum(t + 2 * PIPE + b, PEER_TN - 1), b)
        for b in range(PIPE):
            compute(t + PIPE + b, PIPE + b)
        return carry

    lax.fori_loop(0, PEER_TN // (2 * PIPE), group, 0)


def _sublane_allsum(p):
    p = p + pltpu.roll(p, 4, 0)
    p = p + pltpu.roll(p, 2, 0)
    return p + pltpu.roll(p, 1, 0)


def _load_tile_indices(off_hbm, off_smem, sem):
    i = pl.program_id(0)
    cp = pltpu.make_async_copy(off_hbm.at[pl.ds(i * PEER_TN, PEER_TN)], off_smem, sem)
    cp.start()
    cp.wait()


def _load_table_once(tab_hbm, tab_vmem, sem):
    @pl.when(pl.program_id(0) == 0)
    def _():
        cp = pltpu.make_async_copy(tab_hbm, tab_vmem, sem)
        cp.start()
        cp.wait()


def _gather_scratch(table_shape):
    return [pltpu.VMEM(table_shape, U32), pltpu.SMEM((PEER_TN, PEER_SEL), I32),
            *[pltpu.VMEM((PEER_SEL * SUBLANES, LANES), U32) for _ in range(2 * PIPE)],
            pltpu.VMEM((SUBLANES, STAGE_ROWS), F32), pltpu.VMEM((2 * PEER_SEL, STAGE_ROWS), BF16),
            pltpu.SemaphoreType.DMA((2,))]


def _gather_prologue(off_hbm, tab_hbm, tab_vmem, off_smem, kmask_ref, e8_ref, sems):
    _load_table_once(tab_hbm, tab_vmem, sems.at[0])
    _load_tile_indices(off_hbm, off_smem, sems.at[1])

    @pl.when(pl.program_id(0) == 0)
    def _():
        _layout_tables(kmask_ref, e8_ref)


def _peer_u_kernel(off_hbm, tab_hbm, h3_ref, eid_ref, gate_ref, w2_ref,
                   tab_vmem, off_smem, *rest):
    bufs, (kmask_ref, e8_ref, sems, s_ref) = rest[:2 * PIPE], rest[2 * PIPE:]
    _gather_prologue(off_hbm, tab_hbm, tab_vmem, off_smem, kmask_ref, e8_ref, sems)
    sub = lax.broadcasted_iota(I32, (SUBLANES, STAGE_ROWS), 0)
    s_ref[...] = jnp.zeros(s_ref.shape, F32)

    def compute(t, b):
        h = h3_ref[t]
        h_hi = h.astype(BF16)
        h_lo = (h - h_hi.astype(F32)).astype(BF16)
        r = lax.dot_general(jnp.concatenate([h_hi, h_lo], axis=0), _staged(bufs[b]), NT_DIMS,
                            preferred_element_type=F32)
        tot = _sublane_allsum((r[0:SUBLANES] + r[SUBLANES:]) * kmask_ref[...])
        t8 = pl.multiple_of((t >> 3) << 3, SUBLANES)
        s_ref[pl.ds(t8, SUBLANES), :] = jnp.where(sub == (t & 7), tot, s_ref[pl.ds(t8, SUBLANES), :])

    _token_pipeline(lambda t, b: _stage_token(t, off_smem, tab_vmem, bufs[b]), compute)
    e8 = e8_ref[...]
    act2 = sum(lax.dot_general(sk, e8, NT_DIMS, preferred_element_type=F32) for sk in _split3(s_ref[...]))
    j = lax.broadcasted_iota(I32, (PEER_SEL, 2 * PEER_SEL), 0)
    a = lax.broadcasted_iota(I32, (PEER_SEL, 2 * PEER_SEL), 1)
    dup = ((a >> 1) == j).astype(BF16)
    gate2 = sum(jnp.dot(gk, dup, preferred_element_type=F32) for gk in _split3(gate_ref[...]))
    pair2 = jnp.dot((eid_ref[...] & 1).astype(BF16), dup, preferred_element_type=F32)
    slot_pair = (lax.broadcasted_iota(I32, act2.shape, 1) & 1).astype(F32)
    w2_ref[...] = jnp.where(pair2 == slot_pair, jax.nn.gelu(act2) * gate2, 0.0)


def _peer_u(off_tok, eid_tok, gate_tok, u_packed, h3):
    t = off_tok.shape[0]
    tok = lambda w: pl.BlockSpec((PEER_TN, w), lambda i: (i, 0))
    any_spec = pl.BlockSpec(memory_space=pl.ANY)
    return pl.pallas_call(
        _peer_u_kernel,
        grid=(t // PEER_TN,),
        in_specs=[any_spec, any_spec, pl.BlockSpec((PEER_TN, SUBLANES, LANES), lambda i: (i, 0, 0)),
                  tok(PEER_SEL), tok(PEER_SEL)],
        out_specs=tok(2 * PEER_SEL),
        out_shape=jax.ShapeDtypeStruct((t, 2 * PEER_SEL), F32),
        scratch_shapes=_gather_scratch(u_packed.shape) + [pltpu.VMEM((PEER_TN, STAGE_ROWS), F32)],
        compiler_params=_params(("arbitrary",), 52),
        name="peer_u",
    )(off_tok, u_packed, h3, eid_tok, gate_tok)


def _peer_v_kernel(off_hbm, tab_hbm, w2_ref, x3_ref, gt_ref, o_ref,
                   tab_vmem, off_smem, *rest):
    bufs, (kmask_ref, e8_ref, sems, wx_ref) = rest[:2 * PIPE], rest[2 * PIPE:]
    _gather_prologue(off_hbm, tab_hbm, tab_vmem, off_smem, kmask_ref, e8_ref, sems)
    w2 = w2_ref[...]
    w_hi = w2.astype(BF16)
    w_lo = (w2 - w_hi.astype(F32)).astype(BF16)
    e8 = e8_ref[...]
    wx_ref[0] = jnp.dot(w_hi, e8, preferred_element_type=F32)
    wx_ref[1] = jnp.dot(w_lo, e8, preferred_element_type=F32)
    sub = lax.broadcasted_iota(I32, (SUBLANES, STAGE_ROWS), 0)

    def compute(t, b):
        t8 = pl.multiple_of((t >> 3) << 3, SUBLANES)
        pick = sub == (t & 7)
        kmask = kmask_ref[...]
        rows = []
        for k in range(2):
            w = _sublane_allsum(jnp.where(pick, wx_ref[k, pl.ds(t8, SUBLANES), :], 0.0))
            rows.append((w * kmask).astype(BF16))
        o = jnp.dot(jnp.concatenate(rows, axis=0), _staged(bufs[b]), preferred_element_type=F32)
        o_ref[t] = x3_ref[t] + gt_ref[0] * (o[0:SUBLANES] + o[SUBLANES:])

    _token_pipeline(lambda t, b: _stage_token(t, off_smem, tab_vmem, bufs[b]), compute)


def _peer_v(off_tok, w2, v_packed, x3, gt3, gate_row):
    t = off_tok.shape[0]
    any_spec = pl.BlockSpec(memory_space=pl.ANY)
    tok = pl.BlockSpec((PEER_TN, SUBLANES, LANES), lambda i: (i, 0, 0))
    return pl.pallas_call(
        _peer_v_kernel,
        grid=(t // PEER_TN,),
        in_specs=[any_spec, any_spec, pl.BlockSpec((PEER_TN, 2 * PEER_SEL), lambda i: (i, 0)), tok,
                  pl.BlockSpec((1, SUBLANES, LANES), lambda i: (gate_row(i), 0, 0))],
        out_specs=tok,
        out_shape=jax.ShapeDtypeStruct(x3.shape, F32),
        scratch_shapes=_gather_scratch(v_packed.shape) + [pltpu.VMEM((2, PEER_TN, STAGE_ROWS), F32)],
        compiler_params=_params(("arbitrary",), 52),
        name="peer_v",
    )(off_tok, v_packed, w2, x3, gt3)


def _pack_table(tab):
    n = tab.shape[0]
    bits = lax.bitcast_convert_type(tab.astype(BF16), jnp.uint16).astype(U32)
    words = bits[:, :D // 2] | (bits[:, D // 2:] << 16)
    return words.reshape(n * D // (2 * LANES), LANES)


def _peer(xn, h2, gt3, gate_row, pw):
    b, s, _ = xn.shape
    t = b * s
    eid_sel, off_sel, gate_sel = _peer_scores(h2.reshape(t, D), pw['wq_t'], pw['keys'])
    eid_tok, off_tok, gate_tok = (a.reshape(PEER_SEL, t).T for a in (eid_sel, off_sel, gate_sel))
    w2 = _peer_u(off_tok, eid_tok, gate_tok, pw['u_packed'], h2.reshape(t, SUBLANES, LANES))
    out = _peer_v(off_tok, w2, pw['v_packed'], xn.reshape(t, SUBLANES, LANES), gt3,
                  functools.partial(gate_row, tiles_per_seq=s // PEER_TN))
    return out.reshape(b, s, D)


def _rope_tables(s):
    pos = np.arange(s)
    half = QK_ROPE // 2
    inv = ROPE_BASE ** (-np.arange(0, half, 2, dtype=np.float64) / half)
    ar = (pos // GRID_W)[:, None] * inv
    ac = (pos % GRID_W)[:, None] * inv
    ang = np.concatenate([ar, ar, ac, ac], axis=-1)
    cos = np.ones((s, HP), np.float32)
    sin_a = np.zeros((s, HP), np.float32)
    sin_b = np.zeros((s, HP), np.float32)
    cos[:, QK_NOPE:QK_HEAD] = np.cos(ang)
    sin = np.sin(ang)
    first = (np.arange(QK_ROPE) % (half)) < half // 2
    sin_a[:, QK_NOPE:QK_HEAD] = np.where(first, -sin, 0.0)
    sin_b[:, QK_NOPE:QK_HEAD] = np.where(first, 0.0, sin)
    return jnp.asarray(cos), jnp.asarray(sin_a), jnp.asarray(sin_b)


def _identity_tables(s):
    return jnp.ones((s, HP), F32), jnp.zeros((s, HP), F32), jnp.zeros((s, HP), F32)


def _pad_heads(w, width):
    k = w.shape[0]
    w = w.reshape(k, HEADS, width)
    return jnp.pad(w, ((0, 0), (0, 0), (0, HP - width))).reshape(k, HEADS * HP)


def _pad_lanes(g):
    return jnp.pad(g, (0, HP - g.shape[0])).reshape(1, HP)


def _layer_weights(i, w_in, g_ckv, w_ukv, g_cq, w_uq, g_qn, g_kn, w_dw, b_dw, g_cln, b_cln,
                   wb_attn, wb_fnet, wb_conv, w_out):
    win = w_in[i]
    w_pad = jnp.concatenate([win[:, :KV_IN], jnp.zeros((D, KV_IN_PAD - KV_IN), F32), win[:, KV_IN:]], axis=1)
    ukv = w_ukv[i].reshape(KV_LORA, HEADS, QK_NOPE + V_HEAD)
    wb_a = wb_attn[i].reshape(HEADS, V_HEAD, D)
    return {
        'w_in': w_pad.astype(BF16),
        'g_ckv': g_ckv[i].reshape(1, KV_LORA), 'g_cq': g_cq[i].reshape(1, Q_LORA),
        'g_qn': _pad_lanes(g_qn[i]), 'g_kn': _pad_lanes(g_kn[i]),
        'w_uk': _pad_heads(ukv[:, :, :QK_NOPE].reshape(KV_LORA, HEADS * QK_NOPE), QK_NOPE).astype(BF16),
        'w_uv': _pad_heads(ukv[:, :, QK_NOPE:].reshape(KV_LORA, HEADS * V_HEAD), V_HEAD).astype(BF16),
        'w_uq': _pad_heads(w_uq[i], QK_HEAD).astype(BF16),
        'w_dw': jnp.pad(w_dw[i].reshape(CONV_W, CONV_C), ((0, 1), (0, 0))),
        'b_dw': b_dw[i].reshape(1, CONV_C), 'g_cln': g_cln[i].reshape(1, CONV_C),
        'b_cln': b_cln[i].reshape(1, CONV_C),
        'wb_attn': jnp.pad(wb_a, ((0, 0), (0, HP - V_HEAD), (0, 0))).reshape(HEADS * HP, D).astype(BF16),
        'wb_fnet': wb_fnet[i].astype(BF16), 'wb_conv': wb_conv[i].astype(BF16), 'w_out': w_out[i].astype(BF16),
    }


def _mixer_branches(zf, zc, lw):
    fm = _fourier_latent(zf) if zf.shape[1] == FFT_N * FFT_N else _fourier_direct(zf)
    return fm, _conformer_conv(zc, lw)


def kernel(x, c, ctx, c_ctx, w_mod, b_mod, g_norm1, w_in, g_ckv, w_ukv, g_cq, w_uq, g_qn, g_kn, w_dw, b_dw,
           g_cln, b_cln, wb_attn, wb_fnet, wb_conv, w_out, g_norm2, w_query, sub_keys, u_tab, v_tab):
    b, s, _ = x.shape
    n_ctx = ctx.shape[1]
    depth = w_mod.shape[0]
    ctx_row = b
    cond = jnp.concatenate([c, c_ctx[None, :], jnp.zeros((16 - b - 1, D), F32)], axis=0)
    mods_all = _adaln(cond, w_mod, b_mod)
    rope = _rope_tables(s)
    no_rope = _identity_tables(n_ctx)
    x_row = lambda bi: bi
    c_row = lambda bi: ctx_row
    x_gate_row = lambda i, tiles_per_seq: i // tiles_per_seq
    c_gate_row = lambda i, tiles_per_seq: ctx_row
    pos_tiles = s // 256
    for i in range(depth):
        lw = _layer_weights(i, w_in, g_ckv, w_ukv, g_cq, w_uq, g_qn, g_kn, w_dw, b_dw, g_cln, b_cln,
                            wb_attn, wb_fnet, wb_conv, w_out)
        pw = {'wq_t': w_query[i].T, 'keys': sub_keys[i].reshape(2 * PEER_HEADS, N_KEYS, D_KEY // 2),
              'u_packed': _pack_table(u_tab[i]), 'v_packed': _pack_table(v_tab[i])}
        mods = mods_all[i].reshape(16, 1, 6 * D)
        gt3 = mods_all[i][:, 5 * D:6 * D].reshape(16, SUBLANES, LANES)
        update_ctx = i < depth - 1

        kv_x, cq_x, zf_x, zc_x, zg_x = _inproj(x, mods, x_row, g_norm1[i], lw['w_in'], IN_SEGS)
        if update_ctx:
            kv_c, cq_c, zf_c, zc_c, zg_c = _inproj(ctx, mods, c_row, g_norm1[i], lw['w_in'], IN_SEGS)
        else:
            kv_c, cq_c = _inproj(ctx, mods, c_row, g_norm1[i], lw['w_in'][:, :KV_IN_PAD + Q_LORA], IN_SEGS[:2])
        q_c, k_c, v_c = _mla_prep(kv_c, cq_c, no_rope, lambda t: 0, lw)
        q_x, k_x, v_x = _mla_prep(kv_x, cq_x, rope, lambda t: t % pos_tiles, lw)

        a_x = _attention(q_x, [(k_c, v_c), (k_x, v_x)])
        fm_x, cv_x = _mixer_branches(zf_x, zc_x, lw)
        xn, h2 = _merge(a_x, fm_x, cv_x, zg_x, x, mods, x_row, g_norm2[i], lw)
        x = _peer(xn, h2, gt3, x_gate_row, pw)
        if update_ctx:
            a_c = _attention(q_c, [(k_c, v_c)])
            fm_c, cv_c = _mixer_branches(zf_c, zc_c, lw)
            cn, ch2 = _merge(a_c, fm_c, cv_c, zg_c, ctx, mods, c_row, g_norm2[i], lw)
            ctx = _peer(cn, ch2, gt3, c_gate_row, pw)
    return x
```
